```python
import jax, jax.numpy as jnp
from jax import lax
import numpy as np

D_MODEL = 2048
BATCH = 4
SEQ = 2048
DEPTH = 4
DEC_BATCH = 128
DEC_SEQ = 4
PAST_LEN = 16384
PAGE_SIZE = 128

S5_WIDTH = D_MODEL // 2
S5_GROUP = 16
S5_GROUPS = S5_WIDTH // S5_GROUP
S5_STATE = 64
RG_WIDTH = D_MODEL
RG_BLOCKS = 16
RG_BLOCK = RG_WIDTH // RG_BLOCKS
RG_C = 8.0
CONV_W = 4
IN_COLS = 2 * S5_WIDTH + 2 * RG_WIDTH + 2 * D_MODEL
EPS = 1e-6
DT_MIN = 1e-3
DT_MAX = 1e-1

kernel_name = "hybrid_s5_rglru_gated_step"

_SPLITS = [S5_WIDTH, 2 * S5_WIDTH, 2 * S5_WIDTH + RG_WIDTH, 2 * S5_WIDTH + 2 * RG_WIDTH,
           2 * S5_WIDTH + 2 * RG_WIDTH + D_MODEL]


def _rmsnorm(x, g):
    xf = x.astype(jnp.float32)
    y = xf * lax.rsqrt(jnp.mean(xf * xf, axis=-1, keepdims=True) + EPS)
    return (y * g.astype(jnp.float32)).astype(x.dtype)


def _complex_linear_scan(a_re, a_im, b_re, b_im):
    def combine(l, r):
        a1r, a1i, b1r, b1i = l
        a2r, a2i, b2r, b2i = r
        return (a1r * a2r - a1i * a2i,
                a1r * a2i + a1i * a2r,
                a2r * b1r - a2i * b1i + b2r,
                a2r * b1i + a2i * b1r + b2i)
    return lax.associative_scan(combine, (a_re, a_im, b_re, b_im), axis=1)


def _real_linear_scan(a, b):
    def combine(l, r):
        a1, b1 = l
        a2, b2 = r
        return a1 * a2, a2 * b1 + b2
    return lax.associative_scan(combine, (a, b), axis=1)


def _s5_branch(u, h0_re, h0_im, p):
    f32 = jnp.float32
    bsz, t = u.shape[0], u.shape[1]
    uf = u.astype(f32).reshape(bsz, t, S5_GROUPS, S5_GROUP)
    lam_re = p["s5_lam_re"].astype(f32)
    lam_im = p["s5_lam_im"].astype(f32)
    dt = jnp.exp(p["s5_log_dt"].astype(f32))[:, None]
    mag = jnp.exp(lam_re * dt)
    abar_re = mag * jnp.cos(lam_im * dt)
    abar_im = mag * jnp.sin(lam_im * dt)
    nr = abar_re - 1.0
    ni = abar_im
    den = lam_re * lam_re + lam_im * lam_im
    coef_re = ((nr * lam_re + ni * lam_im) / den)[..., None]
    coef_im = ((ni * lam_re - nr * lam_im) / den)[..., None]
    b_re = p["s5_b_re"].astype(f32)
    b_im = p["s5_b_im"].astype(f32)
    bb_re = coef_re * b_re - coef_im * b_im
    bb_im = coef_re * b_im + coef_im * b_re
    bu_re = jnp.einsum("gpc,btgc->btgp", bb_re, uf)
    bu_im = jnp.einsum("gpc,btgc->btgp", bb_im, uf)
    h0r = h0_re.astype(f32)
    h0i = h0_im.astype(f32)
    bu_re = bu_re.at[:, 0].add(abar_re * h0r - abar_im * h0i)
    bu_im = bu_im.at[:, 0].add(abar_re * h0i + abar_im * h0r)
    a_re = jnp.broadcast_to(abar_re, bu_re.shape)
    a_im = jnp.broadcast_to(abar_im, bu_im.shape)
    _, _, h_re, h_im = _complex_linear_scan(a_re, a_im, bu_re, bu_im)
    y = (jnp.einsum("gcp,btgp->btgc", p["s5_c_re"].astype(f32), h_re)
         - jnp.einsum("gcp,btgp->btgc", p["s5_c_im"].astype(f32), h_im))
    y = y.reshape(bsz, t, S5_WIDTH) + p["s5_d"].astype(f32) * u.astype(f32)
    y = jax.nn.gelu(y, approximate=False)
    y = y * jax.nn.sigmoid(y @ p["s5_w_glu"].astype(f32) + p["s5_b_glu"].astype(f32))
    return y.astype(u.dtype), h_re[:, -1], h_im[:, -1]


def _rglru_branch(xb, h0, conv_buf, p):
    f32 = jnp.float32
    bsz, t = xb.shape[0], xb.shape[1]
    xcat = jnp.concatenate([conv_buf.astype(xb.dtype), xb], axis=1)
    new_buf = xcat[:, -(CONV_W - 1):]
    w = p["rg_conv_w"]
    conv = p["rg_conv_b"] + sum(w[k] * xcat[:, k:k + t] for k in range(CONV_W))
    xh = conv.reshape(bsz, t, RG_BLOCKS, RG_BLOCK)
    r = jax.nn.sigmoid(jnp.einsum("bthi,hij->bthj", xh, p["rg_w_r"]).reshape(bsz, t, RG_WIDTH)
                       + p["rg_b_r"]).astype(f32)
    gi = jax.nn.sigmoid(jnp.einsum("bthi,hij->bthj", xh, p["rg_w_i"]).reshape(bsz, t, RG_WIDTH)
                        + p["rg_b_i"]).astype(f32)
    log_a = -RG_C * r * jax.nn.softplus(-p["rg_lam"].astype(f32))
    a = jnp.exp(log_a)
    mult = jnp.sqrt(-jnp.expm1(2.0 * log_a))
    b = mult * gi * conv.astype(f32)
    b = b.at[:, 0].add(a[:, 0] * h0.astype(f32))
    _, h = _real_linear_scan(a, b)
    return h.astype(xb.dtype), h[:, -1], new_buf


def _layer(x, c, s5_re0, s5_im0, rg_h0, conv0, p):
    ada = c @ p["w_ada"] + p["b_ada"]
    shift, scale, gate = jnp.split(ada, 3, axis=-1)
    xn = _rmsnorm(x, p["norm_gain"]) * (1.0 + scale[:, None]) + shift[:, None]
    proj = xn @ p["w_in"] + p["b_in"]
    u_a, z_a, x_b, z_b, g_a, g_b = jnp.split(proj, _SPLITS, axis=-1)
    y_a, s5_re, s5_im = _s5_branch(u_a, s5_re0, s5_im0, p)
    y_b, rg_h, conv_buf = _rglru_branch(x_b, rg_h0, conv0, p)
    y_a = y_a * jax.nn.silu(z_a)
    y_b = y_b * jax.nn.silu(z_b)
    merged = (jax.nn.sigmoid(g_a) * (y_a @ p["w_proj_a"])
              + jax.nn.sigmoid(g_b) * (y_b @ p["w_proj_b"]))
    out = merged @ p["w_out"]
    x = x + gate[:, None] * out
    return x, (s5_re, s5_im, rg_h, conv_buf)


def _trunk(x, c, s5_re0, s5_im0, rg_h0, conv0, params, final_gain, state_dtype):
    s5r, s5i, rgh, cnv = [], [], [], []
    for l in range(DEPTH):
        p = {k: v[l] for k, v in params.items()}
        x, (a, b, h, buf) = _layer(x, c, s5_re0[l], s5_im0[l], rg_h0[l], conv0[l], p)
        s5r.append(a.astype(state_dtype))
        s5i.append(b.astype(state_dtype))
        rgh.append(h.astype(state_dtype))
        cnv.append(buf.astype(state_dtype))
    y = _rmsnorm(x, final_gain)
    return y, jnp.stack(s5r), jnp.stack(s5i), jnp.stack(rgh), jnp.stack(cnv)


def setup_inputs(seed: int = 0) -> dict:
    key = jax.random.key(seed)
    ks = iter(jax.random.split(key, 40))
    f32 = jnp.float32

    def nrm(shape, s):
        return jax.random.normal(next(ks), shape, f32) * s

    n_idx = jnp.arange(S5_STATE, dtype=f32)
    u_a = jax.random.uniform(next(ks), (DEPTH, RG_WIDTH), f32, 0.9, 0.999)
    sig = u_a ** (1.0 / RG_C)
    rg_lam = jnp.log(sig) - jnp.log1p(-sig)
    log_dt = jax.random.uniform(next(ks), (DEPTH, S5_GROUPS), f32, np.log(DT_MIN), np.log(DT_MAX))
    return {
        "x_prompt": nrm((BATCH, SEQ, D_MODEL), 1.0),
        "x_sample": nrm((DEC_BATCH, DEC_SEQ, D_MODEL), 1.0),
        "state_s5_re": nrm((DEPTH, DEC_BATCH, S5_GROUPS, S5_STATE), 0.1),
        "state_s5_im": nrm((DEPTH, DEC_BATCH, S5_GROUPS, S5_STATE), 0.1),
        "state_rglru_h": nrm((DEPTH, DEC_BATCH, RG_WIDTH), 0.5),
        "state_conv": nrm((DEPTH, DEC_BATCH, CONV_W - 1, RG_WIDTH), 1.0),
        "c_prompt": nrm((BATCH, D_MODEL), 1.0),
        "c_sample": nrm((DEC_BATCH, D_MODEL), 1.0),
        "w_ada": nrm((DEPTH, D_MODEL, 3 * D_MODEL), 0.5 * D_MODEL ** -0.5),
        "b_ada": nrm((DEPTH, 3 * D_MODEL), 0.01),
        "norm_gain": 1.0 + nrm((DEPTH, D_MODEL), 0.01),
        "w_in": nrm((DEPTH, D_MODEL, IN_COLS), D_MODEL ** -0.5),
        "b_in": nrm((DEPTH, IN_COLS), 0.01),
        "s5_lam_re": -0.5 + nrm((DEPTH, S5_GROUPS, S5_STATE), 0.01),
        "s5_lam_im": jnp.pi * n_idx + nrm((DEPTH, S5_GROUPS, S5_STATE), 0.01),
        "s5_log_dt": log_dt,
        "s5_b_re": nrm((DEPTH, S5_GROUPS, S5_STATE, S5_GROUP), (2 * S5_GROUP) ** -0.5),
        "s5_b_im": nrm((DEPTH, S5_GROUPS, S5_STATE, S5_GROUP), (2 * S5_GROUP) ** -0.5),
        "s5_c_re": nrm((DEPTH, S5_GROUPS, S5_GROUP, S5_STATE), S5_STATE ** -0.5),
        "s5_c_im": nrm((DEPTH, S5_GROUPS, S5_GROUP, S5_STATE), S5_STATE ** -0.5),
        "s5_d": nrm((DEPTH, S5_WIDTH), 1.0),
        "s5_w_glu": nrm((DEPTH, S5_WIDTH, S5_WIDTH), S5_WIDTH ** -0.5),
        "s5_b_glu": nrm((DEPTH, S5_WIDTH), 0.01),
        "rg_conv_w": nrm((DEPTH, CONV_W, RG_WIDTH), CONV_W ** -0.5),
        "rg_conv_b": nrm((DEPTH, RG_WIDTH), 0.01),
        "rg_w_r": nrm((DEPTH, RG_BLOCKS, RG_BLOCK, RG_BLOCK), RG_BLOCK ** -0.5),
        "rg_b_r": nrm((DEPTH, RG_WIDTH), 0.01),
        "rg_w_i": nrm((DEPTH, RG_BLOCKS, RG_BLOCK, RG_BLOCK), RG_BLOCK ** -0.5),
        "rg_b_i": nrm((DEPTH, RG_WIDTH), 0.01),
        "rg_lam": rg_lam,
        "w_proj_a": nrm((DEPTH, S5_WIDTH, D_MODEL), S5_WIDTH ** -0.5),
        "w_proj_b": nrm((DEPTH, RG_WIDTH, D_MODEL), RG_WIDTH ** -0.5),
        "w_out": nrm((DEPTH, D_MODEL, D_MODEL), D_MODEL ** -0.5),
        "final_gain": 1.0 + nrm((D_MODEL,), 0.01),
    }


def reference(x_prompt, x_sample, state_s5_re, state_s5_im, state_rglru_h, state_conv,
              c_prompt, c_sample, w_ada, b_ada, norm_gain, w_in, b_in,
              s5_lam_re, s5_lam_im, s5_log_dt, s5_b_re, s5_b_im, s5_c_re, s5_c_im,
              s5_d, s5_w_glu, s5_b_glu, rg_conv_w, rg_conv_b, rg_w_r, rg_b_r, rg_w_i, rg_b_i,
              rg_lam, w_proj_a, w_proj_b, w_out, final_gain):
    params = {
        "w_ada": w_ada, "b_ada": b_ada, "norm_gain": norm_gain, "w_in": w_in, "b_in": b_in,
        "s5_lam_re": s5_lam_re, "s5_lam_im": s5_lam_im, "s5_log_dt": s5_log_dt,
        "s5_b_re": s5_b_re, "s5_b_im": s5_b_im, "s5_c_re": s5_c_re, "s5_c_im": s5_c_im,
        "s5_d": s5_d, "s5_w_glu": s5_w_glu, "s5_b_glu": s5_b_glu,
        "rg_conv_w": rg_conv_w, "rg_conv_b": rg_conv_b, "rg_w_r": rg_w_r, "rg_b_r": rg_b_r,
        "rg_w_i": rg_w_i, "rg_b_i": rg_b_i, "rg_lam": rg_lam,
        "w_proj_a": w_proj_a, "w_proj_b": w_proj_b, "w_out": w_out,
    }
    sdt = state_s5_re.dtype
    bp = x_prompt.shape[0]
    z_s5 = jnp.zeros((DEPTH, bp, S5_GROUPS, S5_STATE), jnp.float32)
    z_h = jnp.zeros((DEPTH, bp, RG_WIDTH), jnp.float32)
    z_conv = jnp.zeros((DEPTH, bp, CONV_W - 1, RG_WIDTH), x_prompt.dtype)
    y_prompt, s5_re_p, s5_im_p, rg_h_p, conv_p = _trunk(
        x_prompt, c_prompt, z_s5, z_s5, z_h, z_conv, params, final_gain, sdt)
    y_sample, s5_re_s, s5_im_s, rg_h_s, conv_s = _trunk(
        x_sample, c_sample, state_s5_re, state_s5_im, state_rglru_h, state_conv, params, final_gain, sdt)
    return (y_prompt, y_sample, s5_re_p, s5_im_p, rg_h_p, conv_p, s5_re_s, s5_im_s, rg_h_s, conv_s)
```

```python
import functools

import jax
import jax.numpy as jnp
from jax import lax
from jax.experimental import pallas as pl
from jax.experimental.pallas import tpu as pltpu

F32 = jnp.float32
BF16 = jnp.bfloat16

D_MODEL = 2048
DEPTH = 4
B_P, T_P = 4, 2048
B_S, T_S = 128, 4
S5_WIDTH = 1024
S5_GROUP = 16
S5_GROUPS = 64
S5_STATE = 64
S5_N = S5_GROUPS * S5_STATE
RG_WIDTH = 2048
RG_BLOCKS = 16
RG_BLOCK = 128
RG_C = 8.0
CONV_W = 4
IN_COLS = 2 * S5_WIDTH + 2 * RG_WIDTH + 2 * D_MODEL
EPS = 1e-6

TM = 512
N_P = B_P * T_P
N_S = B_S * T_S
N_ROWS = N_P + N_S
NP_TILES = N_P // TM
N_TILES = N_ROWS // TM
MOD_ROWS = 128
S5_CHUNKS = 8
S5_CK = S5_WIDTH // S5_CHUNKS
S5_CN = S5_N // S5_CHUNKS
SUB = 8
VMEM_LIMIT = 56 * 1024 * 1024

assert N_S == TM and N_P % TM == 0 and TM % MOD_ROWS == 0


def _cparams(n_axes, vmem=VMEM_LIMIT):
    return pltpu.CompilerParams(dimension_semantics=("arbitrary",) * n_axes,
                                vmem_limit_bytes=vmem)


def _dot(a, b):
    return jnp.dot(a, b, preferred_element_type=F32)


def _roll4(x):
    return pltpu.roll(x, SUB // 2, 0)


def _s5_prep_kernel(lre_ref, lim_ref, ldt_ref, bre_ref, bim_ref,
                    ar_ref, ai_ref, bbre_ref, bbim_ref):
    lam_re = lre_ref[0]
    lam_im = lim_ref[0]
    dt = jnp.exp(ldt_ref[0])
    mag = jnp.exp(lam_re * dt)
    a_re = mag * jnp.cos(lam_im * dt)
    a_im = mag * jnp.sin(lam_im * dt)
    nr = a_re - 1.0
    ni = a_im
    den = lam_re * lam_re + lam_im * lam_im
    c_re = (nr * lam_re + ni * lam_im) / den
    c_im = (ni * lam_re - nr * lam_im) / den
    b_re = bre_ref[0]
    b_im = bim_ref[0]
    bbre_ref[0] = c_re * b_re - c_im * b_im
    bbim_ref[0] = c_re * b_im + c_im * b_re
    ar_ref[0] = a_re
    ai_ref[0] = a_im


def _s5_prep(lam_re, lam_im, log_dt, bt_re, bt_im):
    vec = pl.BlockSpec((1, 1, S5_N), lambda l: (l, 0, 0))
    mat = pl.BlockSpec((1, S5_GROUP, S5_N), lambda l: (l, 0, 0))
    return pl.pallas_call(
        _s5_prep_kernel,
        grid=(DEPTH,),
        in_specs=[vec, vec, vec, mat, mat],
        out_specs=[vec, vec, mat, mat],
        out_shape=[jax.ShapeDtypeStruct((DEPTH, 1, S5_N), F32)] * 2
        + [jax.ShapeDtypeStruct((DEPTH, S5_GROUP, S5_N), F32)] * 2,
        compiler_params=_cparams(1),
        name="s5_prep",
    )(lam_re, lam_im, log_dt, bt_re, bt_im)


ADA_TN = 1024


def _ada_kernel(c_ref, w_ref, b_ref, o_ref):
    o_ref[0] = _dot(c_ref[...].astype(BF16), w_ref[0].astype(BF16)) + b_ref[0]


def _ada(c_all, w_ada, b_ada):
    rows = c_all.shape[0]
    return pl.pallas_call(
        _ada_kernel,
        grid=(DEPTH, 3 * D_MODEL // ADA_TN),
        in_specs=[pl.BlockSpec((rows, D_MODEL), lambda l, j: (0, 0)),
                  pl.BlockSpec((1, D_MODEL, ADA_TN), lambda l, j: (l, 0, j)),
                  pl.BlockSpec((1, 1, ADA_TN), lambda l, j: (l, 0, j))],
        out_specs=pl.BlockSpec((1, rows, ADA_TN), lambda l, j: (l, 0, j)),
        out_shape=jax.ShapeDtypeStruct((DEPTH, rows, 3 * D_MODEL), F32),
        compiler_params=_cparams(2),
        name="ada",
    )(c_all, w_ada, b_ada.reshape(DEPTH, 1, 3 * D_MODEL))


def _mod_index(tm):
    tiles_p = N_P // tm
    return lambda i: (i // tiles_p, 0, 0)


def _rms(x, gain):
    ms = jnp.mean(x * x, axis=-1, keepdims=True)
    return x * lax.rsqrt(ms + EPS) * gain


def _norm_mod_kernel(x_ref, g_ref, sc_ref, sh_ref, o_ref):
    tm = x_ref.shape[0]
    for s in range(tm // MOD_ROWS):
        rows = slice(s * MOD_ROWS, (s + 1) * MOD_ROWS)
        y = _rms(x_ref[rows, :], g_ref[...])
        o_ref[rows, :] = (y * (1.0 + sc_ref[0]) + sh_ref[0]).astype(BF16)


def _norm_mod(x, gain, scale, shift):
    tm = 256
    row = pl.BlockSpec((tm, D_MODEL), lambda i: (i, 0))
    mod = pl.BlockSpec((1, MOD_ROWS, D_MODEL), _mod_index(tm))
    return pl.pallas_call(
        _norm_mod_kernel,
        grid=(N_ROWS // tm,),
        in_specs=[row, pl.BlockSpec((1, D_MODEL), lambda i: (0, 0)), mod, mod],
        out_specs=row,
        out_shape=jax.ShapeDtypeStruct((N_ROWS, D_MODEL), BF16),
        compiler_params=_cparams(1),
        name="norm_mod",
    )(x, gain, scale, shift)


PROJ_TN = 1024


def _proj_kernel(xn_ref, w_ref, b_ref, o_ref, wbf_ref):
    @pl.when(pl.program_id(1) == 0)
    def _():
        wbf_ref[...] = w_ref[...].astype(BF16)

    o_ref[...] = _dot(xn_ref[...], wbf_ref[...]) + b_ref[...]


def _proj(xn, w_in, b_in):
    return pl.pallas_call(
        _proj_kernel,
        grid=(IN_COLS // PROJ_TN, N_TILES),
        in_specs=[pl.BlockSpec((TM, D_MODEL), lambda j, i: (i, 0)),
                  pl.BlockSpec((D_MODEL, PROJ_TN), lambda j, i: (0, j)),
                  pl.BlockSpec((1, PROJ_TN), lambda j, i: (0, j))],
        out_specs=pl.BlockSpec((TM, PROJ_TN), lambda j, i: (i, j)),
        out_shape=jax.ShapeDtypeStruct((N_ROWS, IN_COLS), F32),
        scratch_shapes=[pltpu.VMEM((D_MODEL, PROJ_TN), BF16)],
        compiler_params=_cparams(2),
        name="proj_in",
    )(xn, w_in, b_in.reshape(1, IN_COLS))


def _gelu(y):
    return 0.5 * y * (1.0 + lax.erf(y * (2.0 ** -0.5)))


def _s5_kernel(u_ref, z_ref, wbu_ref, wcre_ref, wcim_ref, ar_ref, ai_ref, d_ref,
               wglu_ref, bglu_ref, h0re_ref, h0im_ref,
               ya_ref, hpre_ref, hpim_ref, hsre_ref, hsim_ref,
               bu_ref, y_ref, xr_ref, xi_ref, wglu_bf_ref):
    i = pl.program_id(0)

    @pl.when(i == 0)
    def _init():
        wglu_bf_ref[...] = wglu_ref[...].astype(BF16)
        xr_ref[...] = jnp.zeros_like(xr_ref)
        xi_ref[...] = jnp.zeros_like(xi_ref)

    def bu_matmul(c):
        ub = u_ref[:, c * S5_CK:(c + 1) * S5_CK].astype(BF16)
        bu_ref[...] = _dot(ub, wbu_ref[c])

    def c_matmul(c):
        hr = bu_ref[:, 0:S5_CN].astype(BF16)
        hi = bu_ref[:, S5_CN:2 * S5_CN].astype(BF16)
        y_ref[:, c * S5_CK:(c + 1) * S5_CK] = _dot(hr, wcre_ref[c]) - _dot(hi, wcim_ref[c])

    def abar(c):
        lanes = slice(c * S5_CN, (c + 1) * S5_CN)
        a_re = jnp.broadcast_to(ar_ref[:, lanes], (SUB, S5_CN))
        a_im = jnp.broadcast_to(ai_ref[:, lanes], (SUB, S5_CN))
        return a_re, a_im

    @pl.when(i < NP_TILES)
    def _prompt():
        lo = lax.broadcasted_iota(jnp.int32, (SUB, S5_CN), 0) < SUB // 2
        for c in range(S5_CHUNKS):
            lanes = slice(c * S5_CN, (c + 1) * S5_CN)
            bu_matmul(c)
            a_re, a_im = abar(c)

            def body(k, carry, a_re=a_re, a_im=a_im):
                xr, xi = carry
                rows = pl.ds(pl.multiple_of(k * SUB, SUB), SUB)
                br = bu_ref[rows, 0:S5_CN]
                bi = bu_ref[rows, S5_CN:2 * S5_CN]
                xr, xi = _roll4(xr), _roll4(xi)
                yr = a_re * xr - a_im * xi + br
                yi = a_re * xi + a_im * xr + bi
                sr, si = _roll4(yr), _roll4(yi)
                zr = a_re * sr - a_im * si + br
                zi = a_re * si + a_im * sr + bi
                hr = jnp.where(lo, yr, zr)
                hi = jnp.where(lo, yi, zi)
                bu_ref[rows, 0:S5_CN] = hr
                bu_ref[rows, S5_CN:2 * S5_CN] = hi
                return hr, hi

            xr, xi = lax.fori_loop(0, TM // SUB, body, (xr_ref[:, lanes], xi_ref[:, lanes]))
            xr_ref[:, lanes] = xr
            xi_ref[:, lanes] = xi
            c_matmul(c)

    @pl.when(i == NP_TILES)
    def _sample():
        for c in range(S5_CHUNKS):
            lanes = slice(c * S5_CN, (c + 1) * S5_CN)
            bu_matmul(c)
            a_re, a_im = abar(c)

            def body(g, carry, a_re=a_re, a_im=a_im, lanes=lanes):
                r0 = pl.multiple_of(g * SUB, SUB)
                xr = h0re_ref[pl.ds(r0, SUB), lanes]
                xi = h0im_ref[pl.ds(r0, SUB), lanes]
                for t in range(T_S):
                    rows = pl.ds(t * B_S + r0, SUB)
                    br = bu_ref[rows, 0:S5_CN]
                    bi = bu_ref[rows, S5_CN:2 * S5_CN]
                    xr, xi = (a_re * xr - a_im * xi + br, a_re * xi + a_im * xr + bi)
                    bu_ref[rows, 0:S5_CN] = xr
                    bu_ref[rows, S5_CN:2 * S5_CN] = xi
                hsre_ref[pl.ds(r0, SUB), lanes] = xr
                hsim_ref[pl.ds(r0, SUB), lanes] = xi
                return carry

            lax.fori_loop(0, B_S // SUB, body, 0)
            c_matmul(c)

    @pl.when(i == NP_TILES - 1)
    def _prompt_state():
        hpre_ref[...] = xr_ref[...]
        hpim_ref[...] = xi_ref[...]

    rc = 64

    def act(j, carry):
        rows = pl.ds(pl.multiple_of(j * rc, rc), rc)
        y_ref[rows, :] = _gelu(y_ref[rows, :] + d_ref[...] * u_ref[rows, :])
        return carry

    lax.fori_loop(0, TM // rc, act, 0)
    bu_ref[...] = _dot(y_ref[...].astype(BF16), wglu_bf_ref[...])

    def gate(j, carry):
        rows = pl.ds(pl.multiple_of(j * rc, rc), rc)
        y = y_ref[rows, :]
        y = y * jax.nn.sigmoid(bu_ref[rows, :] + bglu_ref[...])
        z = z_ref[rows, :]
        ya_ref[rows, :] = (y * (z * jax.nn.sigmoid(z))).astype(BF16)
        return carry

    lax.fori_loop(0, TM // rc, gate, 0)


def _s5_branch(proj, wbu, wcre, wcim, a_re, a_im, d, w_glu, b_glu, h0_re, h0_im):
    const2 = lambda i: (0, 0)
    const3 = lambda i: (0, 0, 0)
    state_p = pl.BlockSpec((SUB, S5_N), const2)
    state_s = pl.BlockSpec((B_S, S5_N), const2)
    vec = pl.BlockSpec((1, S5_WIDTH), const2)
    return pl.pallas_call(
        _s5_kernel,
        grid=(N_TILES,),
        in_specs=[pl.BlockSpec((TM, S5_WIDTH), lambda i: (i, 0)),
                  pl.BlockSpec((TM, S5_WIDTH), lambda i: (i, 1)),
                  pl.BlockSpec((S5_CHUNKS, S5_CK, 2 * S5_CN), const3),
                  pl.BlockSpec((S5_CHUNKS, S5_CN, S5_CK), const3),
                  pl.BlockSpec((S5_CHUNKS, S5_CN, S5_CK), const3),
                  pl.BlockSpec((1, S5_N), const2),
                  pl.BlockSpec((1, S5_N), const2),
                  vec,
                  pl.BlockSpec((S5_WIDTH, S5_WIDTH), const2),
                  vec,
                  state_s, state_s],
        out_specs=[pl.BlockSpec((TM, S5_WIDTH), lambda i: (i, 0)),
                   state_p, state_p, state_s, state_s],
        out_shape=[jax.ShapeDtypeStruct((N_ROWS, S5_WIDTH), BF16),
                   jax.ShapeDtypeStruct((SUB, S5_N), F32),
                   jax.ShapeDtypeStruct((SUB, S5_N), F32),
                   jax.ShapeDtypeStruct((B_S, S5_N), F32),
                   jax.ShapeDtypeStruct((B_S, S5_N), F32)],
        scratch_shapes=[pltpu.VMEM((TM, 2 * S5_CN), F32),
                        pltpu.VMEM((TM, S5_WIDTH), F32),
                        pltpu.VMEM((SUB, S5_N), F32),
                        pltpu.VMEM((SUB, S5_N), F32),
                        pltpu.VMEM((S5_WIDTH, S5_WIDTH), BF16)],
        compiler_params=_cparams(1),
        name="s5_branch",
    )(proj, proj, wbu, wcre, wcim, a_re, a_im, d, w_glu, b_glu, h0_re, h0_im)


RG_LC = 512


def _rg_kernel(xb_ref, zb_ref, cw_ref, cb_ref, wg_ref, br_ref, bi_ref, lam_ref,
               h0_ref, c0_ref,
               yb_ref, hp_ref, hs_ref,
               cv_ref, a_ref, hx_ref, pv_ref, pr_ref, ps_ref):
    i = pl.program_id(0)

    @pl.when(i == 0)
    def _init():
        for r in (hx_ref, pv_ref, pr_ref, ps_ref):
            r[...] = jnp.zeros_like(r)

    lo = lax.broadcasted_iota(jnp.int32, (SUB, RG_LC), 0) < SUB // 2

    def taps(lanes):
        return [jnp.broadcast_to(cw_ref[k:k + 1, lanes], (SUB, RG_LC)) for k in range(CONV_W)]

    @pl.when(i < NP_TILES)
    def _conv_prompt():
        for c in range(RG_WIDTH // RG_LC):
            lanes = slice(c * RG_LC, (c + 1) * RG_LC)
            w0, w1, w2, w3 = taps(lanes)
            cb = jnp.broadcast_to(cb_ref[:, lanes], (SUB, RG_LC))

            def body(k, carry, w0=w0, w1=w1, w2=w2, w3=w3, cb=cb, lanes=lanes):
                v_prev, r_prev, s_prev = carry
                rows = pl.ds(pl.multiple_of(k * SUB, SUB), SUB)
                v = xb_ref[rows, lanes]
                r = _roll4(v)
                s = jnp.where(lo, r_prev, r)
                cv_ref[rows, lanes] = cb + w3 * v + w2 * s + w1 * v_prev + w0 * s_prev
                return v, r, s

            v, r, s = lax.fori_loop(0, TM // SUB, body,
                                    (pv_ref[:, lanes], pr_ref[:, lanes], ps_ref[:, lanes]))
            pv_ref[:, lanes] = v
            pr_ref[:, lanes] = r
            ps_ref[:, lanes] = s

    @pl.when(i == NP_TILES)
    def _conv_sample():
        def src(t, r0, lanes):
            if t >= 0:
                return xb_ref[pl.ds(t * B_S + r0, SUB), lanes]
            return c0_ref[pl.ds((CONV_W - 1 + t) * B_S + r0, SUB), lanes]

        for c in range(RG_WIDTH // RG_LC):
            lanes = slice(c * RG_LC, (c + 1) * RG_LC)
            w0, w1, w2, w3 = taps(lanes)
            cb = jnp.broadcast_to(cb_ref[:, lanes], (SUB, RG_LC))

            def body(g, carry, w0=w0, w1=w1, w2=w2, w3=w3, cb=cb, lanes=lanes):
                r0 = pl.multiple_of(g * SUB, SUB)
                for t in range(T_S):
                    cv_ref[pl.ds(t * B_S + r0, SUB), lanes] = (
                        cb + w3 * src(t, r0, lanes) + w2 * src(t - 1, r0, lanes)
                        + w1 * src(t - 2, r0, lanes) + w0 * src(t - 3, r0, lanes))
                return carry

            lax.fori_loop(0, B_S // SUB, body, 0)

    sp = lam_ref[...]
    c_lam = -RG_C * (jnp.maximum(-sp, 0.0) + jnp.log1p(jnp.exp(-jnp.abs(sp))))
    for h in range(RG_BLOCKS):
        lanes = slice(h * RG_BLOCK, (h + 1) * RG_BLOCK)
        cv = cv_ref[:, lanes]
        g = _dot(cv.astype(BF16), wg_ref[h])
        r = jax.nn.sigmoid(g[:, :RG_BLOCK] + br_ref[:, lanes])
        gi = jax.nn.sigmoid(g[:, RG_BLOCK:] + bi_ref[:, lanes])
        log_a = c_lam[:, lanes] * r
        a = jnp.exp(log_a)
        mult = jnp.sqrt(1.0 - a * a)
        a_ref[:, lanes] = a
        cv_ref[:, lanes] = mult * gi * cv

    @pl.when(i < NP_TILES)
    def _scan_prompt():
        for c in range(RG_WIDTH // RG_LC):
            lanes = slice(c * RG_LC, (c + 1) * RG_LC)

            def body(k, x, lanes=lanes):
                rows = pl.ds(pl.multiple_of(k * SUB, SUB), SUB)
                a = a_ref[rows, lanes]
                b = cv_ref[rows, lanes]
                y = a * _roll4(x) + b
                z = a * _roll4(y) + b
                h = jnp.where(lo, y, z)
                cv_ref[rows, lanes] = h
                return h

            hx_ref[:, lanes] = lax.fori_loop(0, TM // SUB, body, hx_ref[:, lanes])

    @pl.when(i == NP_TILES)
    def _scan_sample():
        for c in range(RG_WIDTH // RG_LC):
            lanes = slice(c * RG_LC, (c + 1) * RG_LC)

            def body(g, carry, lanes=lanes):
                r0 = pl.multiple_of(g * SUB, SUB)
                x = h0_ref[pl.ds(r0, SUB), lanes]
                for t in range(T_S):
                    rows = pl.ds(t * B_S + r0, SUB)
                    x = a_ref[rows, lanes] * x + cv_ref[rows, lanes]
                    cv_ref[rows, lanes] = x
                hs_ref[pl.ds(r0, SUB), lanes] = x
                return carry

            lax.fori_loop(0, B_S // SUB, body, 0)

    @pl.when(i == NP_TILES - 1)
    def _prompt_state():
        hp_ref[...] = hx_ref[...]

    rc = 64

    def out(j, carry):
        rows = pl.ds(pl.multiple_of(j * rc, rc), rc)
        z = zb_ref[rows, :]
        yb_ref[rows, :] = (cv_ref[rows, :] * (z * jax.nn.sigmoid(z))).astype(BF16)
        return carry

    lax.fori_loop(0, TM // rc, out, 0)


def _rg_branch(proj, conv_w, conv_b, wg, b_r, b_i, lam, h0, conv0):
    const2 = lambda i: (0, 0)
    vec = pl.BlockSpec((1, RG_WIDTH), const2)
    carry = pltpu.VMEM((SUB, RG_WIDTH), F32)
    return pl.pallas_call(
        _rg_kernel,
        grid=(N_TILES,),
        in_specs=[pl.BlockSpec((TM, RG_WIDTH), lambda i: (i, 1)),
                  pl.BlockSpec((TM, RG_WIDTH), lambda i: (i, 2)),
                  pl.BlockSpec((CONV_W, RG_WIDTH), const2),
                  vec,
                  pl.BlockSpec((RG_BLOCKS, RG_BLOCK, 2 * RG_BLOCK), lambda i: (0, 0, 0)),
                  vec, vec, vec,
                  pl.BlockSpec((B_S, RG_WIDTH), const2),
                  pl.BlockSpec(((CONV_W - 1) * B_S, RG_WIDTH), const2)],
        out_specs=[pl.BlockSpec((TM, RG_WIDTH), lambda i: (i, 0)),
                   pl.BlockSpec((SUB, RG_WIDTH), const2),
                   pl.BlockSpec((B_S, RG_WIDTH), const2)],
        out_shape=[jax.ShapeDtypeStruct((N_ROWS, RG_WIDTH), BF16),
                   jax.ShapeDtypeStruct((SUB, RG_WIDTH), F32),
                   jax.ShapeDtypeStruct((B_S, RG_WIDTH), F32)],
        scratch_shapes=[pltpu.VMEM((TM, RG_WIDTH), F32),
                        pltpu.VMEM((TM, RG_WIDTH), F32),
                        carry, carry, carry, carry],
        compiler_params=_cparams(1),
        name="rg_branch",
    )(proj, proj, conv_w, conv_b, wg, b_r, b_i, lam, h0, conv0)


MERGE_TN = 512
GA_COL0 = (2 * S5_WIDTH + 2 * RG_WIDTH) // MERGE_TN
GB_COL0 = GA_COL0 + D_MODEL // MERGE_TN


def _merge_kernel(ya_ref, yb_ref, ga_ref, gb_ref, wa_ref, wb_ref, o_ref, wa_bf_ref, wb_bf_ref):
    @pl.when(pl.program_id(1) == 0)
    def _():
        wa_bf_ref[...] = wa_ref[...].astype(BF16)
        wb_bf_ref[...] = wb_ref[...].astype(BF16)

    pa = _dot(ya_ref[...], wa_bf_ref[...])
    pb = _dot(yb_ref[...], wb_bf_ref[...])
    o_ref[...] = (jax.nn.sigmoid(ga_ref[...]) * pa + jax.nn.sigmoid(gb_ref[...]) * pb).astype(BF16)


def _merge(ya, yb, proj, w_a, w_b):
    return pl.pallas_call(
        _merge_kernel,
        grid=(D_MODEL // MERGE_TN, N_TILES),
        in_specs=[pl.BlockSpec((TM, S5_WIDTH), lambda j, i: (i, 0)),
                  pl.BlockSpec((TM, RG_WIDTH), lambda j, i: (i, 0)),
                  pl.BlockSpec((TM, MERGE_TN), lambda j, i: (i, GA_COL0 + j)),
                  pl.BlockSpec((TM, MERGE_TN), lambda j, i: (i, GB_COL0 + j)),
                  pl.BlockSpec((S5_WIDTH, MERGE_TN), lambda j, i: (0, j)),
                  pl.BlockSpec((RG_WIDTH, MERGE_TN), lambda j, i: (0, j))],
        out_specs=pl.BlockSpec((TM, MERGE_TN), lambda j, i: (i, j)),
        out_shape=jax.ShapeDtypeStruct((N_ROWS, D_MODEL), BF16),
        scratch_shapes=[pltpu.VMEM((S5_WIDTH, MERGE_TN), BF16),
                        pltpu.VMEM((RG_WIDTH, MERGE_TN), BF16)],
        compiler_params=_cparams(2),
        name="merge",
    )(ya, yb, proj, proj, w_a, w_b)


OUT_TM = 256


def _out_kernel(last, m_ref, x_ref, w_ref, gate_ref, g_ref, *rest):
    if last:
        y_ref, wbf_ref = rest
    else:
        sc_ref, sh_ref, xo_ref, xn_ref, wbf_ref = rest

    @pl.when(pl.program_id(0) == 0)
    def _():
        wbf_ref[...] = w_ref[...].astype(BF16)

    out = _dot(m_ref[...], wbf_ref[...])
    for s in range(OUT_TM // MOD_ROWS):
        rows = slice(s * MOD_ROWS, (s + 1) * MOD_ROWS)
        x = x_ref[rows, :] + gate_ref[0] * out[rows, :]
        y = _rms(x, g_ref[...])
        if last:
            y_ref[rows, :] = y
        else:
            xo_ref[rows, :] = x
            xn_ref[rows, :] = (y * (1.0 + sc_ref[0]) + sh_ref[0]).astype(BF16)


def _out_layer(merged, x, w_out, gate, gain, scale=None, shift=None):
    last = scale is None
    row = pl.BlockSpec((OUT_TM, D_MODEL), lambda i: (i, 0))
    mod = pl.BlockSpec((1, MOD_ROWS, D_MODEL), _mod_index(OUT_TM))
    in_specs = [row, row,
                pl.BlockSpec((D_MODEL, D_MODEL), lambda i: (0, 0), pipeline_mode=pl.Buffered(1)),
                mod, pl.BlockSpec((1, D_MODEL), lambda i: (0, 0))]
    args = [merged, x, w_out, gate, gain]
    if last:
        out_specs = row
        out_shape = jax.ShapeDtypeStruct((N_ROWS, D_MODEL), F32)
    else:
        in_specs += [mod, mod]
        args += [scale, shift]
        out_specs = [row, row]
        out_shape = [jax.ShapeDtypeStruct((N_ROWS, D_MODEL), F32),
                     jax.ShapeDtypeStruct((N_ROWS, D_MODEL), BF16)]
    return pl.pallas_call(
        functools.partial(_out_kernel, last),
        grid=(N_ROWS // OUT_TM,),
        in_specs=in_specs,
        out_specs=out_specs,
        out_shape=out_shape,
        scratch_shapes=[pltpu.VMEM((D_MODEL, D_MODEL), BF16)],
        compiler_params=_cparams(1),
        name="out_last" if last else "out_layer",
    )(*args)


def _time_major(x):
    b, t, d = x.shape
    return x.transpose(1, 0, 2).reshape(t * b, d)


def _batch_major(rows, b, t):
    return rows.reshape(t, b, rows.shape[-1]).transpose(1, 0, 2)


def _block_diag_in(bbt):
    gpc = S5_GROUPS // S5_CHUNKS
    w = bbt.reshape(S5_GROUP, S5_CHUNKS, gpc, S5_STATE)
    eye = jnp.eye(gpc, dtype=bbt.dtype)
    w = jnp.einsum("kcgp,gh->cgkhp", w, eye)
    return w.reshape(S5_CHUNKS, S5_CK, S5_CN)


def _block_diag_out(cmat):
    gpc = S5_GROUPS // S5_CHUNKS
    w = cmat.reshape(S5_CHUNKS, gpc, S5_GROUP, S5_STATE)
    eye = jnp.eye(gpc, dtype=cmat.dtype)
    w = jnp.einsum("cgkp,gh->chpgk", w, eye)
    return w.reshape(S5_CHUNKS, S5_CN, S5_CK)


def _mod_pattern(ada_rows):
    p = jnp.tile(ada_rows[:B_P], (MOD_ROWS // B_P, 1))
    s = ada_rows[B_P:B_P + B_S]
    return jnp.stack([p, s])


def kernel(x_prompt, x_sample, state_s5_re, state_s5_im, state_rglru_h, state_conv,
           c_prompt, c_sample, w_ada, b_ada, norm_gain, w_in, b_in,
           s5_lam_re, s5_lam_im, s5_log_dt, s5_b_re, s5_b_im, s5_c_re, s5_c_im,
           s5_d, s5_w_glu, s5_b_glu, rg_conv_w, rg_conv_b, rg_w_r, rg_b_r, rg_w_i, rg_b_i,
           rg_lam, w_proj_a, w_proj_b, w_out, final_gain):
    sdt = state_s5_re.dtype
    x = jnp.concatenate([_time_major(x_prompt), _time_major(x_sample)], axis=0)

    to_lanes = lambda v: v.reshape(DEPTH, 1, S5_N)
    bt = lambda b: b.reshape(DEPTH, S5_N, S5_GROUP).transpose(0, 2, 1)
    log_dt = jnp.repeat(s5_log_dt, S5_STATE, axis=-1)
    a_re, a_im, bbt_re, bbt_im = _s5_prep(to_lanes(s5_lam_re), to_lanes(s5_lam_im),
                                          to_lanes(log_dt), bt(s5_b_re), bt(s5_b_im))

    n_c = B_P + B_S
    c_all = jnp.concatenate([c_prompt, c_sample, jnp.zeros((-n_c % SUB, D_MODEL), F32)], axis=0)
    ada = _ada(c_all, w_ada, b_ada)

    def mods(l):
        shift, scale, gate = jnp.split(ada[l], 3, axis=-1)
        return _mod_pattern(shift), _mod_pattern(scale), _mod_pattern(gate)

    shift, scale, gate = mods(0)
    xn = _norm_mod(x, norm_gain[0:1], scale, shift)

    outs = {k: [] for k in ("s5re_p", "s5im_p", "h_p", "conv_p", "s5re_s", "s5im_s", "h_s", "conv_s")}
    y = None
    for l in range(DEPTH):
        proj = _proj(xn, w_in[l], b_in[l])

        wbu = jnp.concatenate([_block_diag_in(bbt_re[l]), _block_diag_in(bbt_im[l])],
                              axis=-1).astype(BF16)
        wcre = _block_diag_out(s5_c_re[l]).astype(BF16)
        wcim = _block_diag_out(s5_c_im[l]).astype(BF16)
        ya, hp_re, hp_im, hs_re, hs_im = _s5_branch(
            proj, wbu, wcre, wcim, a_re[l], a_im[l], s5_d[l:l + 1], s5_w_glu[l], s5_b_glu[l:l + 1],
            state_s5_re[l].reshape(B_S, S5_N), state_s5_im[l].reshape(B_S, S5_N))

        wg = jnp.concatenate([rg_w_r[l], rg_w_i[l]], axis=-1).astype(BF16)
        conv0 = state_conv[l].transpose(1, 0, 2).reshape((CONV_W - 1) * B_S, RG_WIDTH)
        yb, hp, hs = _rg_branch(proj, rg_conv_w[l], rg_conv_b[l:l + 1], wg,
                                rg_b_r[l:l + 1], rg_b_i[l:l + 1], rg_lam[l:l + 1],
                                state_rglru_h[l], conv0)

        merged = _merge(ya, yb, proj, w_proj_a[l], w_proj_b[l])

        if l + 1 < DEPTH:
            shift_n, scale_n, gate_n = mods(l + 1)
            x, xn = _out_layer(merged, x, w_out[l], gate, norm_gain[l + 1:l + 2], scale_n, shift_n)
            gate = gate_n
        else:
            y = _out_layer(merged, x, w_out[l], gate, final_gain.reshape(1, D_MODEL))

        half = SUB // 2
        xb_cols = slice(2 * S5_WIDTH, 2 * S5_WIDTH + RG_WIDTH)
        tail = CONV_W - 1
        outs["s5re_p"].append(hp_re[half:].reshape(B_P, S5_GROUPS, S5_STATE))
        outs["s5im_p"].append(hp_im[half:].reshape(B_P, S5_GROUPS, S5_STATE))
        outs["h_p"].append(hp[half:])
        outs["conv_p"].append(_batch_major(proj[N_P - tail * B_P:N_P, xb_cols], B_P, tail))
        outs["s5re_s"].append(hs_re.reshape(B_S, S5_GROUPS, S5_STATE))
        outs["s5im_s"].append(hs_im.reshape(B_S, S5_GROUPS, S5_STATE))
        outs["h_s"].append(hs)
        outs["conv_s"].append(_batch_major(proj[N_ROWS - tail * B_S:, xb_cols], B_S, tail))

    st = lambda k: jnp.stack(outs[k]).astype(sdt)
    y_prompt = _batch_major(y[:N_P], B_P, T_P)
    y_sample = _batch_major(y[N_P:], B_S, T_S)
    return (y_prompt, y_sample, st("s5re_p"), st("s5im_p"), st("h_p"), st("conv_p"),
            st("s5re_s"), st("s5im_s"), st("h_s"), st("conv_s"))
```

```python
import functools

import jax
import jax.numpy as jnp
from jax import lax
from jax.experimental import pallas as pl
from jax.experimental.pallas import tpu as pltpu

F32 = jnp.float32
BF16 = jnp.bfloat16

D_MODEL = 2048
DEPTH = 4
B_P, T_P = 4, 2048
B_S, T_S = 128, 4
S5_WIDTH = 1024
S5_GROUP = 16
S5_GROUPS = 64
S5_STATE = 64
S5_N = S5_GROUPS * S5_STATE
RG_WIDTH = 2048
RG_BLOCKS = 16
RG_BLOCK = 128
RG_C = 8.0
CONV_W = 4
IN_COLS = 2 * S5_WIDTH + 2 * RG_WIDTH + 2 * D_MODEL
EPS = 1e-6

TM = 512
N_P = B_P * T_P
N_S = B_S * T_S
N_ROWS = N_P + N_S
NP_TILES = N_P // TM
N_TILES = N_ROWS // TM
MOD_ROWS = 128
S5_CHUNKS = 8
S5_CK = S5_WIDTH // S5_CHUNKS
S5_CN = S5_N // S5_CHUNKS
SUB = 8
VMEM_LIMIT = 56 * 1024 * 1024

assert N_S == TM and N_P % TM == 0 and TM % MOD_ROWS == 0


def _cparams(n_axes, vmem=VMEM_LIMIT):
    return pltpu.CompilerParams(dimension_semantics=("arbitrary",) * n_axes,
                                vmem_limit_bytes=vmem)


def _dot(a, b):
    return jnp.dot(a, b, preferred_element_type=F32)


def _roll4(x):
    return pltpu.roll(x, SUB // 2, 0)


def _sigmoid(x):
    return 0.5 * jnp.tanh(0.5 * x) + 0.5


def _lvec(a):
    return a.reshape(DEPTH, 1, a.shape[-1])


def _lspec(l, shape, n_axes=1):
    zeros = (0,) * len(shape)
    if n_axes == 1:
        return pl.BlockSpec((1,) + tuple(shape), lambda i: (l,) + zeros)
    return pl.BlockSpec((1,) + tuple(shape), lambda j, i: (l,) + zeros)


def _s5_prep_kernel(lre_ref, lim_ref, ldt_ref, bre_ref, bim_ref,
                    ar_ref, ai_ref, bbre_ref, bbim_ref):
    lam_re = lre_ref[0]
    lam_im = lim_ref[0]
    dt = jnp.exp(ldt_ref[0])
    mag = jnp.exp(lam_re * dt)
    a_re = mag * jnp.cos(lam_im * dt)
    a_im = mag * jnp.sin(lam_im * dt)
    nr = a_re - 1.0
    ni = a_im
    den = lam_re * lam_re + lam_im * lam_im
    c_re = (nr * lam_re + ni * lam_im) / den
    c_im = (ni * lam_re - nr * lam_im) / den
    b_re = bre_ref[0]
    b_im = bim_ref[0]
    bbre_ref[0] = c_re * b_re - c_im * b_im
    bbim_ref[0] = c_re * b_im + c_im * b_re
    ar_ref[0] = a_re
    ai_ref[0] = a_im


def _s5_prep(lam_re, lam_im, log_dt, bt_re, bt_im):
    vec = pl.BlockSpec((1, 1, S5_N), lambda l: (l, 0, 0))
    mat = pl.BlockSpec((1, S5_GROUP, S5_N), lambda l: (l, 0, 0))
    return pl.pallas_call(
        _s5_prep_kernel,
        grid=(DEPTH,),
        in_specs=[vec, vec, vec, mat, mat],
        out_specs=[vec, vec, mat, mat],
        out_shape=[jax.ShapeDtypeStruct((DEPTH, 1, S5_N), F32)] * 2
        + [jax.ShapeDtypeStruct((DEPTH, S5_GROUP, S5_N), F32)] * 2,
        compiler_params=_cparams(1),
        name="s5_prep",
    )(lam_re, lam_im, log_dt, bt_re, bt_im)


ADA_TN = 1024


def _ada_kernel(c_ref, w_ref, b_ref, o_ref):
    o_ref[0] = _dot(c_ref[...].astype(BF16), w_ref[0].astype(BF16)) + b_ref[0]


def _ada(c_all, w_ada, b_ada):
    rows = c_all.shape[0]
    return pl.pallas_call(
        _ada_kernel,
        grid=(DEPTH, 3 * D_MODEL // ADA_TN),
        in_specs=[pl.BlockSpec((rows, D_MODEL), lambda l, j: (0, 0)),
                  pl.BlockSpec((1, D_MODEL, ADA_TN), lambda l, j: (l, 0, j)),
                  pl.BlockSpec((1, 1, ADA_TN), lambda l, j: (l, 0, j))],
        out_specs=pl.BlockSpec((1, rows, ADA_TN), lambda l, j: (l, 0, j)),
        out_shape=jax.ShapeDtypeStruct((DEPTH, rows, 3 * D_MODEL), F32),
        compiler_params=_cparams(2),
        name="ada",
    )(c_all, w_ada, _lvec(b_ada))


MOD_SHIFT, MOD_SCALE, MOD_GATE = 0, 1, 2


def _mod_spec(l, kind, tm):
    tiles_p = N_P // tm
    return pl.BlockSpec((1, 1, 1, MOD_ROWS, D_MODEL), lambda i: (l, kind, i // tiles_p, 0, 0))


def _rms(x, gain):
    ms = jnp.mean(x * x, axis=-1, keepdims=True)
    return x * lax.rsqrt(ms + EPS) * gain


def _norm_mod_kernel(x_ref, g_ref, sc_ref, sh_ref, o_ref):
    tm = x_ref.shape[0]
    for s in range(tm // MOD_ROWS):
        rows = slice(s * MOD_ROWS, (s + 1) * MOD_ROWS)
        y = _rms(x_ref[rows, :], g_ref[0])
        o_ref[rows, :] = (y * (1.0 + sc_ref[0, 0, 0]) + sh_ref[0, 0, 0]).astype(BF16)


def _norm_mod(x, gain, mods, l):
    tm = 256
    row = pl.BlockSpec((tm, D_MODEL), lambda i: (i, 0))
    return pl.pallas_call(
        _norm_mod_kernel,
        grid=(N_ROWS // tm,),
        in_specs=[row, _lspec(l, (1, D_MODEL)),
                  _mod_spec(l, MOD_SCALE, tm), _mod_spec(l, MOD_SHIFT, tm)],
        out_specs=row,
        out_shape=jax.ShapeDtypeStruct((N_ROWS, D_MODEL), BF16),
        compiler_params=_cparams(1),
        name="norm_mod",
    )(x, gain, mods, mods)


PROJ_TN = 1024


def _proj_kernel(xn_ref, w_ref, b_ref, o_ref, wbf_ref):
    @pl.when(pl.program_id(1) == 0)
    def _():
        wbf_ref[...] = w_ref[0].astype(BF16)

    o_ref[...] = _dot(xn_ref[...], wbf_ref[...]) + b_ref[0]


def _proj(xn, w_in, b_in, l):
    return pl.pallas_call(
        _proj_kernel,
        grid=(IN_COLS // PROJ_TN, N_TILES),
        in_specs=[pl.BlockSpec((TM, D_MODEL), lambda j, i: (i, 0)),
                  pl.BlockSpec((1, D_MODEL, PROJ_TN), lambda j, i: (l, 0, j)),
                  pl.BlockSpec((1, 1, PROJ_TN), lambda j, i: (l, 0, j))],
        out_specs=pl.BlockSpec((TM, PROJ_TN), lambda j, i: (i, j)),
        out_shape=jax.ShapeDtypeStruct((N_ROWS, IN_COLS), F32),
        scratch_shapes=[pltpu.VMEM((D_MODEL, PROJ_TN), BF16)],
        compiler_params=_cparams(2),
        name="proj_in",
    )(xn, w_in, b_in)


def _gelu(y):
    return 0.5 * y * (1.0 + lax.erf(y * (2.0 ** -0.5)))


def _s5_kernel(u_ref, z_ref, wbu_ref, wcre_ref, wcim_ref, ar_ref, ai_ref, d_ref,
               wglu_ref, bglu_ref, h0re_ref, h0im_ref,
               ya_ref, hpre_ref, hpim_ref, hsre_ref, hsim_ref,
               bu_ref, y_ref, xr_ref, xi_ref, wglu_bf_ref):
    i = pl.program_id(0)

    @pl.when(i == 0)
    def _init():
        wglu_bf_ref[...] = wglu_ref[0].astype(BF16)
        xr_ref[...] = jnp.zeros_like(xr_ref)
        xi_ref[...] = jnp.zeros_like(xi_ref)

    def bu_matmul(c):
        ub = u_ref[:, c * S5_CK:(c + 1) * S5_CK].astype(BF16)
        bu_ref[...] = _dot(ub, wbu_ref[0, c])

    def c_matmul(c):
        hr = bu_ref[:, 0:S5_CN].astype(BF16)
        hi = bu_ref[:, S5_CN:2 * S5_CN].astype(BF16)
        y_ref[:, c * S5_CK:(c + 1) * S5_CK] = (_dot(hr, wcre_ref[0, c])
                                               - _dot(hi, wcim_ref[0, c]))

    def abar(c):
        lanes = slice(c * S5_CN, (c + 1) * S5_CN)
        a_re = jnp.broadcast_to(ar_ref[0, :, lanes], (SUB, S5_CN))
        a_im = jnp.broadcast_to(ai_ref[0, :, lanes], (SUB, S5_CN))
        return a_re, a_im

    @pl.when(i < NP_TILES)
    def _prompt():
        lo = lax.broadcasted_iota(jnp.int32, (SUB, S5_CN), 0) < SUB // 2
        for c in range(S5_CHUNKS):
            lanes = slice(c * S5_CN, (c + 1) * S5_CN)
            bu_matmul(c)
            a_re, a_im = abar(c)

            def body(k, carry, a_re=a_re, a_im=a_im):
                xr, xi = carry
                rows = pl.ds(pl.multiple_of(k * SUB, SUB), SUB)
                br = bu_ref[rows, 0:S5_CN]
                bi = bu_ref[rows, S5_CN:2 * S5_CN]
                xr, xi = _roll4(xr), _roll4(xi)
                yr = a_re * xr - a_im * xi + br
                yi = a_re * xi + a_im * xr + bi
                sr, si = _roll4(yr), _roll4(yi)
                zr = a_re * sr - a_im * si + br
                zi = a_re * si + a_im * sr + bi
                hr = jnp.where(lo, yr, zr)
                hi = jnp.where(lo, yi, zi)
                bu_ref[rows, 0:S5_CN] = hr
                bu_ref[rows, S5_CN:2 * S5_CN] = hi
                return hr, hi

            xr, xi = lax.fori_loop(0, TM // SUB, body, (xr_ref[:, lanes], xi_ref[:, lanes]))
            xr_ref[:, lanes] = xr
            xi_ref[:, lanes] = xi
            c_matmul(c)

    @pl.when(i == NP_TILES)
    def _sample():
        for c in range(S5_CHUNKS):
            lanes = slice(c * S5_CN, (c + 1) * S5_CN)
            bu_matmul(c)
            a_re, a_im = abar(c)

            def body(g, carry, a_re=a_re, a_im=a_im, lanes=lanes):
                r0 = pl.multiple_of(g * SUB, SUB)
                xr = h0re_ref[0, pl.ds(r0, SUB), lanes]
                xi = h0im_ref[0, pl.ds(r0, SUB), lanes]
                for t in range(T_S):
                    rows = pl.ds(t * B_S + r0, SUB)
                    br = bu_ref[rows, 0:S5_CN]
                    bi = bu_ref[rows, S5_CN:2 * S5_CN]
                    xr, xi = (a_re * xr - a_im * xi + br, a_re * xi + a_im * xr + bi)
                    bu_ref[rows, 0:S5_CN] = xr
                    bu_ref[rows, S5_CN:2 * S5_CN] = xi
                hsre_ref[pl.ds(r0, SUB), lanes] = xr
                hsim_ref[pl.ds(r0, SUB), lanes] = xi
                return carry

            lax.fori_loop(0, B_S // SUB, body, 0)
            c_matmul(c)

    @pl.when(i == NP_TILES - 1)
    def _prompt_state():
        hpre_ref[...] = xr_ref[...]
        hpim_ref[...] = xi_ref[...]

    rc = 64

    def act(j, carry):
        rows = pl.ds(pl.multiple_of(j * rc, rc), rc)
        y_ref[rows, :] = _gelu(y_ref[rows, :] + d_ref[0] * u_ref[rows, :])
        return carry

    lax.fori_loop(0, TM // rc, act, 0)
    bu_ref[...] = _dot(y_ref[...].astype(BF16), wglu_bf_ref[...])

    def gate(j, carry):
        rows = pl.ds(pl.multiple_of(j * rc, rc), rc)
        y = y_ref[rows, :]
        y = y * _sigmoid(bu_ref[rows, :] + bglu_ref[0])
        z = z_ref[rows, :]
        ya_ref[rows, :] = (y * (z * _sigmoid(z))).astype(BF16)
        return carry

    lax.fori_loop(0, TM // rc, gate, 0)


def _s5_branch(proj, wbu, wcre, wcim, a_re, a_im, d, w_glu, b_glu, h0_re, h0_im, l):
    const2 = lambda i: (0, 0)
    state_p = pl.BlockSpec((SUB, S5_N), const2)
    state_s = pl.BlockSpec((B_S, S5_N), const2)
    return pl.pallas_call(
        _s5_kernel,
        grid=(N_TILES,),
        in_specs=[pl.BlockSpec((TM, S5_WIDTH), lambda i: (i, 0)),
                  pl.BlockSpec((TM, S5_WIDTH), lambda i: (i, 1)),
                  _lspec(l, (S5_CHUNKS, S5_CK, 2 * S5_CN)),
                  _lspec(l, (S5_CHUNKS, S5_CN, S5_CK)),
                  _lspec(l, (S5_CHUNKS, S5_CN, S5_CK)),
                  _lspec(l, (1, S5_N)),
                  _lspec(l, (1, S5_N)),
                  _lspec(l, (1, S5_WIDTH)),
                  _lspec(l, (S5_WIDTH, S5_WIDTH)),
                  _lspec(l, (1, S5_WIDTH)),
                  _lspec(l, (B_S, S5_N)),
                  _lspec(l, (B_S, S5_N))],
        out_specs=[pl.BlockSpec((TM, S5_WIDTH), lambda i: (i, 0)),
                   state_p, state_p, state_s, state_s],
        out_shape=[jax.ShapeDtypeStruct((N_ROWS, S5_WIDTH), BF16),
                   jax.ShapeDtypeStruct((SUB, S5_N), F32),
                   jax.ShapeDtypeStruct((SUB, S5_N), F32),
                   jax.ShapeDtypeStruct((B_S, S5_N), F32),
                   jax.ShapeDtypeStruct((B_S, S5_N), F32)],
        scratch_shapes=[pltpu.VMEM((TM, 2 * S5_CN), F32),
                        pltpu.VMEM((TM, S5_WIDTH), F32),
                        pltpu.VMEM((SUB, S5_N), F32),
                        pltpu.VMEM((SUB, S5_N), F32),
                        pltpu.VMEM((S5_WIDTH, S5_WIDTH), BF16)],
        compiler_params=_cparams(1),
        name="s5_branch",
    )(proj, proj, wbu, wcre, wcim, a_re, a_im, d, w_glu, b_glu, h0_re, h0_im)


RG_LC = 512


def _rg_kernel(xb_ref, zb_ref, cw_ref, cb_ref, wg_ref, br_ref, bi_ref, lam_ref,
               h0_ref, c0_ref,
               yb_ref, hp_ref, hs_ref,
               cv_ref, a_ref, hx_ref, pv_ref, pr_ref, ps_ref):
    i = pl.program_id(0)

    @pl.when(i == 0)
    def _init():
        for r in (hx_ref, pv_ref, pr_ref, ps_ref):
            r[...] = jnp.zeros_like(r)

    lo = lax.broadcasted_iota(jnp.int32, (SUB, RG_LC), 0) < SUB // 2

    def taps(lanes):
        return [jnp.broadcast_to(cw_ref[0, k:k + 1, lanes], (SUB, RG_LC)) for k in range(CONV_W)]

    @pl.when(i < NP_TILES)
    def _conv_prompt():
        for c in range(RG_WIDTH // RG_LC):
            lanes = slice(c * RG_LC, (c + 1) * RG_LC)
            w0, w1, w2, w3 = taps(lanes)
            cb = jnp.broadcast_to(cb_ref[0, :, lanes], (SUB, RG_LC))

            def body(k, carry, w0=w0, w1=w1, w2=w2, w3=w3, cb=cb, lanes=lanes):
                v_prev, r_prev, s_prev = carry
                rows = pl.ds(pl.multiple_of(k * SUB, SUB), SUB)
                v = xb_ref[rows, lanes]
                r = _roll4(v)
                s = jnp.where(lo, r_prev, r)
                cv_ref[rows, lanes] = cb + w3 * v + w2 * s + w1 * v_prev + w0 * s_prev
                return v, r, s

            v, r, s = lax.fori_loop(0, TM // SUB, body,
                                    (pv_ref[:, lanes], pr_ref[:, lanes], ps_ref[:, lanes]))
            pv_ref[:, lanes] = v
            pr_ref[:, lanes] = r
            ps_ref[:, lanes] = s

    @pl.when(i == NP_TILES)
    def _conv_sample():
        def src(t, r0, lanes):
            if t >= 0:
                return xb_ref[pl.ds(t * B_S + r0, SUB), lanes]
            return c0_ref[0, pl.ds((CONV_W - 1 + t) * B_S + r0, SUB), lanes]

        for c in range(RG_WIDTH // RG_LC):
            lanes = slice(c * RG_LC, (c + 1) * RG_LC)
            w0, w1, w2, w3 = taps(lanes)
            cb = jnp.broadcast_to(cb_ref[0, :, lanes], (SUB, RG_LC))

            def body(g, carry, w0=w0, w1=w1, w2=w2, w3=w3, cb=cb, lanes=lanes):
                r0 = pl.multiple_of(g * SUB, SUB)
                for t in range(T_S):
                    cv_ref[pl.ds(t * B_S + r0, SUB), lanes] = (
                        cb + w3 * src(t, r0, lanes) + w2 * src(t - 1, r0, lanes)
                        + w1 * src(t - 2, r0, lanes) + w0 * src(t - 3, r0, lanes))
                return carry

            lax.fori_loop(0, B_S // SUB, body, 0)

    sp = lam_ref[0]
    c_lam = -RG_C * (jnp.maximum(-sp, 0.0) + jnp.log1p(jnp.exp(-jnp.abs(sp))))
    for h in range(RG_BLOCKS):
        lanes = slice(h * RG_BLOCK, (h + 1) * RG_BLOCK)
        cv = cv_ref[:, lanes]
        g = _dot(cv.astype(BF16), wg_ref[0, h])
        r = _sigmoid(g[:, :RG_BLOCK] + br_ref[0, :, lanes])
        gi = _sigmoid(g[:, RG_BLOCK:] + bi_ref[0, :, lanes])
        log_a = c_lam[:, lanes] * r
        a = jnp.exp(log_a)
        mult = jnp.sqrt(1.0 - a * a)
        a_ref[:, lanes] = a
        cv_ref[:, lanes] = mult * gi * cv

    @pl.when(i < NP_TILES)
    def _scan_prompt():
        for c in range(RG_WIDTH // RG_LC):
            lanes = slice(c * RG_LC, (c + 1) * RG_LC)

            def body(k, x, lanes=lanes):
                rows = pl.ds(pl.multiple_of(k * SUB, SUB), SUB)
                a = a_ref[rows, lanes]
                b = cv_ref[rows, lanes]
                y = a * _roll4(x) + b
                z = a * _roll4(y) + b
                h = jnp.where(lo, y, z)
                cv_ref[rows, lanes] = h
                return h

            hx_ref[:, lanes] = lax.fori_loop(0, TM // SUB, body, hx_ref[:, lanes])

    @pl.when(i == NP_TILES)
    def _scan_sample():
        for c in range(RG_WIDTH // RG_LC):
            lanes = slice(c * RG_LC, (c + 1) * RG_LC)

            def body(g, carry, lanes=lanes):
                r0 = pl.multiple_of(g * SUB, SUB)
                x = h0_ref[0, pl.ds(r0, SUB), lanes]
                for t in range(T_S):
                    rows = pl.ds(t * B_S + r0, SUB)
                    x = a_ref[rows, lanes] * x + cv_ref[rows, lanes]
                    cv_ref[rows, lanes] = x
                hs_ref[pl.ds(r0, SUB), lanes] = x
                return carry

            lax.fori_loop(0, B_S // SUB, body, 0)

    @pl.when(i == NP_TILES - 1)
    def _prompt_state():
        hp_ref[...] = hx_ref[...]

    rc = 64

    def out(j, carry):
        rows = pl.ds(pl.multiple_of(j * rc, rc), rc)
        z = zb_ref[rows, :]
        yb_ref[rows, :] = (cv_ref[rows, :] * (z * _sigmoid(z))).astype(BF16)
        return carry

    lax.fori_loop(0, TM // rc, out, 0)


def _rg_branch(proj, conv_w, conv_b, wg, b_r, b_i, lam, h0, conv0, l):
    const2 = lambda i: (0, 0)
    vec = _lspec(l, (1, RG_WIDTH))
    carry = pltpu.VMEM((SUB, RG_WIDTH), F32)
    return pl.pallas_call(
        _rg_kernel,
        grid=(N_TILES,),
        in_specs=[pl.BlockSpec((TM, RG_WIDTH), lambda i: (i, 1)),
                  pl.BlockSpec((TM, RG_WIDTH), lambda i: (i, 2)),
                  _lspec(l, (CONV_W, RG_WIDTH)),
                  vec,
                  _lspec(l, (RG_BLOCKS, RG_BLOCK, 2 * RG_BLOCK)),
                  vec, vec, vec,
                  _lspec(l, (B_S, RG_WIDTH)),
                  _lspec(l, ((CONV_W - 1) * B_S, RG_WIDTH))],
        out_specs=[pl.BlockSpec((TM, RG_WIDTH), lambda i: (i, 0)),
                   pl.BlockSpec((SUB, RG_WIDTH), const2),
                   pl.BlockSpec((B_S, RG_WIDTH), const2)],
        out_shape=[jax.ShapeDtypeStruct((N_ROWS, RG_WIDTH), BF16),
                   jax.ShapeDtypeStruct((SUB, RG_WIDTH), F32),
                   jax.ShapeDtypeStruct((B_S, RG_WIDTH), F32)],
        scratch_shapes=[pltpu.VMEM((TM, RG_WIDTH), F32),
                        pltpu.VMEM((TM, RG_WIDTH), F32),
                        carry, carry, carry, carry],
        compiler_params=_cparams(1),
        name="rg_branch",
    )(proj, proj, conv_w, conv_b, wg, b_r, b_i, lam, h0, conv0)


MERGE_TN = 512
GA_COL0 = (2 * S5_WIDTH + 2 * RG_WIDTH) // MERGE_TN
GB_COL0 = GA_COL0 + D_MODEL // MERGE_TN


def _merge_kernel(ya_ref, yb_ref, ga_ref, gb_ref, wa_ref, wb_ref, o_ref, wa_bf_ref, wb_bf_ref):
    @pl.when(pl.program_id(1) == 0)
    def _():
        wa_bf_ref[...] = wa_ref[0].astype(BF16)
        wb_bf_ref[...] = wb_ref[0].astype(BF16)

    pa = _dot(ya_ref[...], wa_bf_ref[...])
    pb = _dot(yb_ref[...], wb_bf_ref[...])
    o_ref[...] = (_sigmoid(ga_ref[...]) * pa + _sigmoid(gb_ref[...]) * pb).astype(BF16)


def _merge(ya, yb, proj, w_a, w_b, l):
    return pl.pallas_call(
        _merge_kernel,
        grid=(D_MODEL // MERGE_TN, N_TILES),
        in_specs=[pl.BlockSpec((TM, S5_WIDTH), lambda j, i: (i, 0)),
                  pl.BlockSpec((TM, RG_WIDTH), lambda j, i: (i, 0)),
                  pl.BlockSpec((TM, MERGE_TN), lambda j, i: (i, GA_COL0 + j)),
                  pl.BlockSpec((TM, MERGE_TN), lambda j, i: (i, GB_COL0 + j)),
                  pl.BlockSpec((1, S5_WIDTH, MERGE_TN), lambda j, i: (l, 0, j)),
                  pl.BlockSpec((1, RG_WIDTH, MERGE_TN), lambda j, i: (l, 0, j))],
        out_specs=pl.BlockSpec((TM, MERGE_TN), lambda j, i: (i, j)),
        out_shape=jax.ShapeDtypeStruct((N_ROWS, D_MODEL), BF16),
        scratch_shapes=[pltpu.VMEM((S5_WIDTH, MERGE_TN), BF16),
                        pltpu.VMEM((RG_WIDTH, MERGE_TN), BF16)],
        compiler_params=_cparams(2),
        name="merge",
    )(ya, yb, proj, proj, w_a, w_b)


OUT_TM = 256


def _out_kernel(last, m_ref, x_ref, w_ref, gate_ref, g_ref, *rest):
    if last:
        y_ref, wbf_ref = rest
    else:
        sc_ref, sh_ref, xo_ref, xn_ref, wbf_ref = rest

    @pl.when(pl.program_id(0) == 0)
    def _():
        wbf_ref[...] = w_ref[0].astype(BF16)

    out = _dot(m_ref[...], wbf_ref[...])
    for s in range(OUT_TM // MOD_ROWS):
        rows = slice(s * MOD_ROWS, (s + 1) * MOD_ROWS)
        x = x_ref[rows, :] + gate_ref[0, 0, 0] * out[rows, :]
        y = _rms(x, g_ref[0])
        if last:
            y_ref[rows, :] = y
        else:
            xo_ref[rows, :] = x
            xn_ref[rows, :] = (y * (1.0 + sc_ref[0, 0, 0]) + sh_ref[0, 0, 0]).astype(BF16)


def _out_layer(merged, x, w_out, mods, l, gain, gain_l):
    last = l + 1 == DEPTH
    row = pl.BlockSpec((OUT_TM, D_MODEL), lambda i: (i, 0))
    in_specs = [row, row,
                pl.BlockSpec((1, D_MODEL, D_MODEL), lambda i: (l, 0, 0),
                             pipeline_mode=pl.Buffered(1)),
                _mod_spec(l, MOD_GATE, OUT_TM),
                _lspec(gain_l, (1, D_MODEL))]
    args = [merged, x, w_out, mods, gain]
    if last:
        out_specs = row
        out_shape = jax.ShapeDtypeStruct((N_ROWS, D_MODEL), F32)
    else:
        in_specs += [_mod_spec(l + 1, MOD_SCALE, OUT_TM), _mod_spec(l + 1, MOD_SHIFT, OUT_TM)]
        args += [mods, mods]
        out_specs = [row, row]
        out_shape = [jax.ShapeDtypeStruct((N_ROWS, D_MODEL), F32),
                     jax.ShapeDtypeStruct((N_ROWS, D_MODEL), BF16)]
    return pl.pallas_call(
        functools.partial(_out_kernel, last),
        grid=(N_ROWS // OUT_TM,),
        in_specs=in_specs,
        out_specs=out_specs,
        out_shape=out_shape,
        scratch_shapes=[pltpu.VMEM((D_MODEL, D_MODEL), BF16)],
        compiler_params=_cparams(1),
        name="out_last" if last else "out_layer",
    )(*args)


def _time_major(x):
    b, t, d = x.shape
    return x.transpose(1, 0, 2).reshape(t * b, d)


def _batch_major(rows, b, t):
    lead = rows.shape[:-2]
    return jnp.swapaxes(rows.reshape(lead + (t, b, rows.shape[-1])), -3, -2)


def _block_diag_in(bbt):
    gpc = S5_GROUPS // S5_CHUNKS
    w = bbt.reshape(DEPTH, S5_GROUP, S5_CHUNKS, gpc, S5_STATE)
    eye = jnp.eye(gpc, dtype=bbt.dtype)
    w = jnp.einsum("lkcgp,gh->lcgkhp", w, eye)
    return w.reshape(DEPTH, S5_CHUNKS, S5_CK, S5_CN)


def _block_diag_out(cmat):
    gpc = S5_GROUPS // S5_CHUNKS
    w = cmat.reshape(DEPTH, S5_CHUNKS, gpc, S5_GROUP, S5_STATE)
    eye = jnp.eye(gpc, dtype=cmat.dtype)
    w = jnp.einsum("lcgkp,gh->lchpgk", w, eye)
    return w.reshape(DEPTH, S5_CHUNKS, S5_CN, S5_CK)


def _mod_patterns(ada):
    a = ada.reshape(DEPTH, ada.shape[1], 3, D_MODEL).transpose(0, 2, 1, 3)
    p = jnp.tile(a[:, :, :B_P], (1, 1, MOD_ROWS // B_P, 1))
    s = a[:, :, B_P:B_P + B_S]
    return jnp.stack([p, s], axis=2)


def kernel(x_prompt, x_sample, state_s5_re, state_s5_im, state_rglru_h, state_conv,
           c_prompt, c_sample, w_ada, b_ada, norm_gain, w_in, b_in,
           s5_lam_re, s5_lam_im, s5_log_dt, s5_b_re, s5_b_im, s5_c_re, s5_c_im,
           s5_d, s5_w_glu, s5_b_glu, rg_conv_w, rg_conv_b, rg_w_r, rg_b_r, rg_w_i, rg_b_i,
           rg_lam, w_proj_a, w_proj_b, w_out, final_gain):
    sdt = state_s5_re.dtype
    x = jnp.concatenate([_time_major(x_prompt), _time_major(x_sample)], axis=0)

    to_lanes = lambda v: v.reshape(DEPTH, 1, S5_N)
    bt = lambda b: b.reshape(DEPTH, S5_N, S5_GROUP).transpose(0, 2, 1)
    log_dt = jnp.repeat(s5_log_dt, S5_STATE, axis=-1)
    a_re, a_im, bbt_re, bbt_im = _s5_prep(to_lanes(s5_lam_re), to_lanes(s5_lam_im),
                                          to_lanes(log_dt), bt(s5_b_re), bt(s5_b_im))
    wbu = jnp.concatenate([_block_diag_in(bbt_re), _block_diag_in(bbt_im)], axis=-1).astype(BF16)
    wcre = _block_diag_out(s5_c_re).astype(BF16)
    wcim = _block_diag_out(s5_c_im).astype(BF16)
    wg = jnp.concatenate([rg_w_r, rg_w_i], axis=-1).astype(BF16)

    n_c = B_P + B_S
    c_all = jnp.concatenate([c_prompt, c_sample, jnp.zeros((-n_c % SUB, D_MODEL), F32)], axis=0)
    mods = _mod_patterns(_ada(c_all, w_ada, b_ada))

    h0_re = state_s5_re.reshape(DEPTH, B_S, S5_N)
    h0_im = state_s5_im.reshape(DEPTH, B_S, S5_N)
    conv0 = state_conv.transpose(0, 2, 1, 3).reshape(DEPTH, (CONV_W - 1) * B_S, RG_WIDTH)
    gains = jnp.concatenate([norm_gain, final_gain[None]], axis=0).reshape(DEPTH + 1, 1, D_MODEL)

    xn = _norm_mod(x, gains, mods, 0)

    outs = {k: [] for k in ("s5re_p", "s5im_p", "h_p", "conv_p", "s5re_s", "s5im_s", "h_s", "conv_s")}
    y = None
    half = SUB // 2
    xb_cols = slice(2 * S5_WIDTH, 2 * S5_WIDTH + RG_WIDTH)
    tail = CONV_W - 1
    for l in range(DEPTH):
        proj = _proj(xn, w_in, _lvec(b_in), l)
        ya, hp_re, hp_im, hs_re, hs_im = _s5_branch(
            proj, wbu, wcre, wcim, a_re, a_im, _lvec(s5_d), s5_w_glu, _lvec(s5_b_glu),
            h0_re, h0_im, l)
        yb, hp, hs = _rg_branch(proj, rg_conv_w, _lvec(rg_conv_b), wg, _lvec(rg_b_r),
                                _lvec(rg_b_i), _lvec(rg_lam), state_rglru_h, conv0, l)
        merged = _merge(ya, yb, proj, w_proj_a, w_proj_b, l)
        res = _out_layer(merged, x, w_out, mods, l, gains, l + 1)
        if l + 1 < DEPTH:
            x, xn = res
        else:
            y = res

        outs["s5re_p"].append(hp_re[half:])
        outs["s5im_p"].append(hp_im[half:])
        outs["h_p"].append(hp[half:])
        outs["conv_p"].append(proj[N_P - tail * B_P:N_P, xb_cols])
        outs["s5re_s"].append(hs_re)
        outs["s5im_s"].append(hs_im)
        outs["h_s"].append(hs)
        outs["conv_s"].append(proj[N_ROWS - tail * B_S:, xb_cols])

    st = lambda k: jnp.stack(outs[k]).astype(sdt)
    s5_shape = lambda b: (DEPTH, b, S5_GROUPS, S5_STATE)
    y_prompt = _batch_major(y[:N_P], B_P, T_P)
    y_sample = _batch_major(y[N_P:], B_S, T_S)
    return (y_prompt, y_sample,
            st("s5re_p").reshape(s5_shape(B_P)), st("s5im_p").reshape(s5_shape(B_P)),
            st("h_p"), _batch_major(st("conv_p"), B_P, tail),
            st("s5re_s").reshape(s5_shape(B_S)), st("s5im_s").reshape(s5_shape(B_S)),
            st("h_s"), _batch_major(st("conv_s"), B_S, tail))
```

```python
import functools

import jax
import jax.numpy as jnp
from jax import lax
from jax.experimental import pallas as pl
from jax.experimental.pallas import tpu as pltpu

F32 = jnp.float32
BF16 = jnp.bfloat16

D_MODEL = 2048
DEPTH = 4
B_P, T_P = 4, 2048
B_S, T_S = 128, 4
S5_WIDTH = 1024
S5_GROUP = 16
S5_GROUPS = 64
S5_STATE = 64
S5_N = S5_GROUPS * S5_STATE
RG_WIDTH = 2048
RG_BLOCKS = 16
RG_BLOCK = 128
RG_C = 8.0
CONV_W = 4
IN_COLS = 2 * S5_WIDTH + 2 * RG_WIDTH + 2 * D_MODEL
EPS = 1e-6

TM = 512
N_P = B_P * T_P
N_S = B_S * T_S
N_ROWS = N_P + N_S
NP_TILES = N_P // TM
N_TILES = N_ROWS // TM
MOD_ROWS = 128
S5_CHUNKS = 8
S5_CK = S5_WIDTH // S5_CHUNKS
S5_CN = S5_N // S5_CHUNKS
SUB = 8
VMEM_LIMIT = 56 * 1024 * 1024

assert N_S == TM and N_P % TM == 0 and TM % MOD_ROWS == 0


def _cparams(n_axes, vmem=VMEM_LIMIT):
    return pltpu.CompilerParams(dimension_semantics=("arbitrary",) * n_axes,
                                vmem_limit_bytes=vmem)


def _dot(a, b):
    return jnp.dot(a, b, preferred_element_type=F32)


def _roll4(x):
    return pltpu.roll(x, SUB // 2, 0)


def _sigmoid(x):
    return 0.5 * jnp.tanh(0.5 * x) + 0.5


def _lvec(a):
    return a.reshape(DEPTH, 1, a.shape[-1])


def _lspec(l, shape, n_axes=1):
    zeros = (0,) * len(shape)
    if n_axes == 1:
        return pl.BlockSpec((1,) + tuple(shape), lambda i: (l,) + zeros)
    return pl.BlockSpec((1,) + tuple(shape), lambda j, i: (l,) + zeros)


def _s5_prep_kernel(lre_ref, lim_ref, ldt_ref, bre_ref, bim_ref,
                    ar_ref, ai_ref, bbre_ref, bbim_ref):
    lam_re = lre_ref[0]
    lam_im = lim_ref[0]
    dt = jnp.exp(ldt_ref[0])
    mag = jnp.exp(lam_re * dt)
    a_re = mag * jnp.cos(lam_im * dt)
    a_im = mag * jnp.sin(lam_im * dt)
    nr = a_re - 1.0
    ni = a_im
    den = lam_re * lam_re + lam_im * lam_im
    c_re = (nr * lam_re + ni * lam_im) / den
    c_im = (ni * lam_re - nr * lam_im) / den
    b_re = bre_ref[0]
    b_im = bim_ref[0]
    bbre_ref[0] = c_re * b_re - c_im * b_im
    bbim_ref[0] = c_re * b_im + c_im * b_re
    ar_ref[0] = a_re
    ai_ref[0] = a_im


def _s5_prep(lam_re, lam_im, log_dt, bt_re, bt_im):
    vec = pl.BlockSpec((1, 1, S5_N), lambda l: (l, 0, 0))
    mat = pl.BlockSpec((1, S5_GROUP, S5_N), lambda l: (l, 0, 0))
    return pl.pallas_call(
        _s5_prep_kernel,
        grid=(DEPTH,),
        in_specs=[vec, vec, vec, mat, mat],
        out_specs=[vec, vec, mat, mat],
        out_shape=[jax.ShapeDtypeStruct((DEPTH, 1, S5_N), F32)] * 2
        + [jax.ShapeDtypeStruct((DEPTH, S5_GROUP, S5_N), F32)] * 2,
        compiler_params=_cparams(1),
        name="s5_prep",
    )(lam_re, lam_im, log_dt, bt_re, bt_im)


ADA_TN = 1024


def _ada_kernel(c_ref, w_ref, b_ref, o_ref):
    o_ref[0] = _dot(c_ref[...].astype(BF16), w_ref[0].astype(BF16)) + b_ref[0]


def _ada(c_all, w_ada, b_ada):
    rows = c_all.shape[0]
    return pl.pallas_call(
        _ada_kernel,
        grid=(DEPTH, 3 * D_MODEL // ADA_TN),
        in_specs=[pl.BlockSpec((rows, D_MODEL), lambda l, j: (0, 0)),
                  pl.BlockSpec((1, D_MODEL, ADA_TN), lambda l, j: (l, 0, j)),
                  pl.BlockSpec((1, 1, ADA_TN), lambda l, j: (l, 0, j))],
        out_specs=pl.BlockSpec((1, rows, ADA_TN), lambda l, j: (l, 0, j)),
        out_shape=jax.ShapeDtypeStruct((DEPTH, rows, 3 * D_MODEL), F32),
        compiler_params=_cparams(2),
        name="ada",
    )(c_all, w_ada, _lvec(b_ada))


MOD_SHIFT, MOD_SCALE, MOD_GATE = 0, 1, 2


def _mod_spec(l, kind, tm):
    tiles_p = N_P // tm
    return pl.BlockSpec((1, 1, 1, MOD_ROWS, D_MODEL), lambda i: (l, kind, i // tiles_p, 0, 0))


def _rms(x, gain):
    ms = jnp.mean(x * x, axis=-1, keepdims=True)
    return x * lax.rsqrt(ms + EPS) * gain


IO_TM = 256
IO_TT = IO_TM // B_P
IO_TILES_P = N_P // IO_TM

_prompt_io_spec = pl.BlockSpec((B_P, IO_TT, D_MODEL), lambda i: (0, jnp.minimum(i, IO_TILES_P - 1), 0))
_sample_io_spec = pl.BlockSpec((IO_TM, D_MODEL), lambda i: (jnp.maximum(i - IO_TILES_P, 0), 0))


def _embed_kernel(xp_ref, xs_ref, g_ref, sc_ref, sh_ref, xo_ref, xn_ref):
    i = pl.program_id(0)

    @pl.when(i < IO_TILES_P)
    def _prompt():
        xo_ref[...] = pltpu.einshape("btd->tbd", xp_ref[...]).reshape(IO_TM, D_MODEL)

    @pl.when(i >= IO_TILES_P)
    def _sample():
        xo_ref[...] = xs_ref[...]

    for s in range(IO_TM // MOD_ROWS):
        rows = slice(s * MOD_ROWS, (s + 1) * MOD_ROWS)
        y = _rms(xo_ref[rows, :], g_ref[0])
        xn_ref[rows, :] = (y * (1.0 + sc_ref[0, 0, 0]) + sh_ref[0, 0, 0]).astype(BF16)


def _embed(x_prompt, x_sample, gain, mods):
    row = pl.BlockSpec((IO_TM, D_MODEL), lambda i: (i, 0))
    return pl.pallas_call(
        _embed_kernel,
        grid=(N_ROWS // IO_TM,),
        in_specs=[_prompt_io_spec, _sample_io_spec, _lspec(0, (1, D_MODEL)),
                  _mod_spec(0, MOD_SCALE, IO_TM), _mod_spec(0, MOD_SHIFT, IO_TM)],
        out_specs=[row, row],
        out_shape=[jax.ShapeDtypeStruct((N_ROWS, D_MODEL), F32),
                   jax.ShapeDtypeStruct((N_ROWS, D_MODEL), BF16)],
        compiler_params=_cparams(1),
        name="embed",
    )(x_prompt, x_sample, gain, mods, mods)


PROJ_TN = 1024
PROJ_TM = N_ROWS // 8
assert N_ROWS % PROJ_TM == 0 and PROJ_TM % (2 * SUB) == 0


def _proj_kernel(xn_ref, w_ref, b_ref, o_ref, wbf_ref):
    @pl.when(pl.program_id(1) == 0)
    def _():
        wbf_ref[...] = w_ref[0].astype(BF16)

    o_ref[...] = _dot(xn_ref[...], wbf_ref[...]) + b_ref[0]


def _proj(xn, w_in, b_in, l):
    return pl.pallas_call(
        _proj_kernel,
        grid=(IN_COLS // PROJ_TN, N_ROWS // PROJ_TM),
        in_specs=[pl.BlockSpec((PROJ_TM, D_MODEL), lambda j, i: (i, 0)),
                  pl.BlockSpec((1, D_MODEL, PROJ_TN), lambda j, i: (l, 0, j)),
                  pl.BlockSpec((1, 1, PROJ_TN), lambda j, i: (l, 0, j))],
        out_specs=pl.BlockSpec((PROJ_TM, PROJ_TN), lambda j, i: (i, j)),
        out_shape=jax.ShapeDtypeStruct((N_ROWS, IN_COLS), F32),
        scratch_shapes=[pltpu.VMEM((D_MODEL, PROJ_TN), BF16)],
        compiler_params=_cparams(2),
        name="proj_in",
    )(xn, w_in, b_in)


def _gelu(y):
    return 0.5 * y * (1.0 + lax.erf(y * (2.0 ** -0.5)))


def _s5_kernel(u_ref, z_ref, wbu_ref, wc_ref, ar_ref, ai_ref, d_ref,
               wglu_ref, bglu_ref, h0re_ref, h0im_ref,
               ya_ref, hp_ref, hsre_ref, hsim_ref,
               bu_ref, y_ref, g_ref, p_ref, wglu_bf_ref):
    i = pl.program_id(0)
    half = SUB // 2

    @pl.when(i == 0)
    def _init():
        wglu_bf_ref[...] = wglu_ref[0].astype(BF16)
        p_ref[...] = jnp.zeros_like(p_ref)

    def abar(c):
        lanes = slice(c * S5_CN, (c + 1) * S5_CN)
        a_re = jnp.broadcast_to(ar_ref[0, :, lanes], (SUB, S5_CN))
        a_im = jnp.broadcast_to(ai_ref[0, :, lanes], (SUB, S5_CN))
        return a_re, a_im

    @pl.when(i < NP_TILES)
    def _prompt():
        groups = TM // SUB
        lo3 = lax.broadcasted_iota(jnp.int32, (groups, SUB, S5_CK), 1) < half
        lo = lax.broadcasted_iota(jnp.int32, (SUB, S5_CN), 0) < half
        for c in range(S5_CHUNKS):
            lanes = slice(c * S5_CN, (c + 1) * S5_CN)
            chans = slice(c * S5_CK, (c + 1) * S5_CK)
            buf = bu_ref.at[c % 2]
            v = u_ref[:, chans].reshape(groups, SUB, S5_CK)
            r = pltpu.roll(v, half, 1)
            zero = jnp.zeros_like(v)
            t_even = jnp.concatenate([jnp.where(lo3, v, zero), jnp.where(lo3, zero, r)], axis=-1)
            t_odd = jnp.concatenate([jnp.where(lo3, r, zero), jnp.where(lo3, zero, v)], axis=-1)
            u2 = jnp.stack([t_even, t_odd], axis=1).reshape(2 * TM, 2 * S5_CK)
            buf[...] = _dot(u2.astype(BF16), wbu_ref[0, c])

            a_re, a_im = abar(c)
            a_sw = jnp.where(lo, -a_im, a_im)
            p = p_ref[:, lanes]
            for t in range(TM // B_P):
                rows = slice(t * SUB, (t + 1) * SUB)
                p = a_re * p + a_sw * _roll4(p) + buf[rows, :]
                buf[rows, :] = p
            p_ref[:, lanes] = p

            q = _dot(buf[...].astype(BF16), wc_ref[0, c])
            q = q.reshape(2 * groups, SUB, 2 * S5_CK)
            dd = q[:, :, :S5_CK] - pltpu.roll(q[:, :, S5_CK:], half, 1)
            dd = dd.reshape(groups, 2, SUB, S5_CK)
            yv = jnp.where(lo3, dd[:, 0], pltpu.roll(dd[:, 1], half, 1))
            y_ref[:, chans] = yv.reshape(TM, S5_CK)

    @pl.when(i == NP_TILES)
    def _sample():
        for c in range(S5_CHUNKS):
            lanes = slice(c * S5_CN, (c + 1) * S5_CN)
            chans = slice(c * S5_CK, (c + 1) * S5_CK)
            buf = bu_ref.at[c % 2]
            ub = u_ref[:, chans].astype(BF16)
            buf[0:TM, :] = _dot(ub, wbu_ref[0, c, 0:S5_CK, :])
            buf[TM:2 * TM, :] = _dot(ub, wbu_ref[0, c, S5_CK:2 * S5_CK, :])
            a_re, a_im = abar(c)

            def body(g, carry, a_re=a_re, a_im=a_im, lanes=lanes, buf=buf):
                r0 = pl.multiple_of(g * SUB, SUB)
                xr = h0re_ref[0, pl.ds(r0, SUB), lanes]
                xi = h0im_ref[0, pl.ds(r0, SUB), lanes]
                for t in range(T_S):
                    rre = pl.ds(t * B_S + r0, SUB)
                    rim = pl.ds(TM + t * B_S + r0, SUB)
                    xr, xi = (a_re * xr - a_im * xi + buf[rre, :], a_re * xi + a_im * xr + buf[rim, :])
                    buf[rre, :] = xr
                    buf[rim, :] = xi
                hsre_ref[pl.ds(r0, SUB), lanes] = xr
                hsim_ref[pl.ds(r0, SUB), lanes] = xi
                return carry

            lax.fori_loop(0, B_S // SUB, body, 0)
            y_ref[:, chans] = (_dot(buf[0:TM, :].astype(BF16), wc_ref[0, c, :, 0:S5_CK])
                               - _dot(buf[TM:2 * TM, :].astype(BF16), wc_ref[0, c, :, S5_CK:2 * S5_CK]))

    @pl.when(i == NP_TILES - 1)
    def _prompt_state():
        hp_ref[...] = p_ref[...]

    rc = 64

    def act(j, carry):
        rows = pl.ds(pl.multiple_of(j * rc, rc), rc)
        y_ref[rows, :] = _gelu(y_ref[rows, :] + d_ref[0] * u_ref[rows, :])
        return carry

    lax.fori_loop(0, TM // rc, act, 0)
    g_ref[...] = _dot(y_ref[...].astype(BF16), wglu_bf_ref[...])

    def gate(j, carry):
        rows = pl.ds(pl.multiple_of(j * rc, rc), rc)
        y = y_ref[rows, :]
        y = y * _sigmoid(g_ref[rows, :] + bglu_ref[0])
        z = z_ref[rows, :]
        ya_ref[rows, :] = (y * (z * _sigmoid(z))).astype(BF16)
        return carry

    lax.fori_loop(0, TM // rc, gate, 0)


def _s5_branch(proj, wbu, wc, a_re, a_im, d, w_glu, b_glu, h0_re, h0_im, l):
    const2 = lambda i: (0, 0)
    state_s = pl.BlockSpec((B_S, S5_N), const2)
    once = dict(pipeline_mode=pl.Buffered(1))
    lconst = lambda shape: pl.BlockSpec((1,) + shape, lambda i: (l,) + (0,) * len(shape), **once)
    return pl.pallas_call(
        _s5_kernel,
        grid=(N_TILES,),
        in_specs=[pl.BlockSpec((TM, S5_WIDTH), lambda i: (i, 0)),
                  pl.BlockSpec((TM, S5_WIDTH), lambda i: (i, 1)),
                  lconst((S5_CHUNKS, 2 * S5_CK, S5_CN)),
                  lconst((S5_CHUNKS, S5_CN, 2 * S5_CK)),
                  _lspec(l, (1, S5_N)),
                  _lspec(l, (1, S5_N)),
                  _lspec(l, (1, S5_WIDTH)),
                  lconst((S5_WIDTH, S5_WIDTH)),
                  _lspec(l, (1, S5_WIDTH)),
                  lconst((B_S, S5_N)),
                  lconst((B_S, S5_N))],
        out_specs=[pl.BlockSpec((TM, S5_WIDTH), lambda i: (i, 0)),
                   pl.BlockSpec((SUB, S5_N), const2), state_s, state_s],
        out_shape=[jax.ShapeDtypeStruct((N_ROWS, S5_WIDTH), BF16),
                   jax.ShapeDtypeStruct((SUB, S5_N), F32),
                   jax.ShapeDtypeStruct((B_S, S5_N), F32),
                   jax.ShapeDtypeStruct((B_S, S5_N), F32)],
        scratch_shapes=[pltpu.VMEM((2, 2 * TM, S5_CN), F32),
                        pltpu.VMEM((TM, S5_WIDTH), F32),
                        pltpu.VMEM((TM, S5_WIDTH), F32),
                        pltpu.VMEM((SUB, S5_N), F32),
                        pltpu.VMEM((S5_WIDTH, S5_WIDTH), BF16)],
        compiler_params=_cparams(1),
        name="s5_branch",
    )(proj, proj, wbu, wc, a_re, a_im, d, w_glu, b_glu, h0_re, h0_im)


RG_LC = 512


def _rg_kernel(xb_ref, zb_ref, cw_ref, cb_ref, wg_ref, br_ref, bi_ref, lam_ref,
               h0_ref, c0_ref,
               yb_ref, hp_ref, hs_ref,
               cv_ref, a_ref, hx_ref, pv_ref, pr_ref, ps_ref):
    i = pl.program_id(0)

    @pl.when(i == 0)
    def _init():
        for r in (hx_ref, pv_ref, pr_ref, ps_ref):
            r[...] = jnp.zeros_like(r)

    lo = lax.broadcasted_iota(jnp.int32, (SUB, RG_LC), 0) < SUB // 2

    def taps(lanes):
        return [jnp.broadcast_to(cw_ref[0, k:k + 1, lanes], (SUB, RG_LC)) for k in range(CONV_W)]

    @pl.when(i < NP_TILES)
    def _conv_prompt():
        for c in range(RG_WIDTH // RG_LC):
            lanes = slice(c * RG_LC, (c + 1) * RG_LC)
            w0, w1, w2, w3 = taps(lanes)
            cb = jnp.broadcast_to(cb_ref[0, :, lanes], (SUB, RG_LC))

            def body(k, carry, w0=w0, w1=w1, w2=w2, w3=w3, cb=cb, lanes=lanes):
                v_prev, r_prev, s_prev = carry
                rows = pl.ds(pl.multiple_of(k * SUB, SUB), SUB)
                v = xb_ref[rows, lanes]
                r = _roll4(v)
                s = jnp.where(lo, r_prev, r)
                cv_ref[rows, lanes] = cb + w3 * v + w2 * s + w1 * v_prev + w0 * s_prev
                return v, r, s

            v, r, s = lax.fori_loop(0, TM // SUB, body,
                                    (pv_ref[:, lanes], pr_ref[:, lanes], ps_ref[:, lanes]))
            pv_ref[:, lanes] = v
            pr_ref[:, lanes] = r
            ps_ref[:, lanes] = s

    @pl.when(i == NP_TILES)
    def _conv_sample():
        def src(t, r0, lanes):
            if t >= 0:
                return xb_ref[pl.ds(t * B_S + r0, SUB), lanes]
            return c0_ref[0, pl.ds((CONV_W - 1 + t) * B_S + r0, SUB), lanes]

        for c in range(RG_WIDTH // RG_LC):
            lanes = slice(c * RG_LC, (c + 1) * RG_LC)
            w0, w1, w2, w3 = taps(lanes)
            cb = jnp.broadcast_to(cb_ref[0, :, lanes], (SUB, RG_LC))

            def body(g, carry, w0=w0, w1=w1, w2=w2, w3=w3, cb=cb, lanes=lanes):
                r0 = pl.multiple_of(g * SUB, SUB)
                for t in range(T_S):
                    cv_ref[pl.ds(t * B_S + r0, SUB), lanes] = (
                        cb + w3 * src(t, r0, lanes) + w2 * src(t - 1, r0, lanes)
                        + w1 * src(t - 2, r0, lanes) + w0 * src(t - 3, r0, lanes))
                return carry

            lax.fori_loop(0, B_S // SUB, body, 0)

    sp = lam_ref[0]
    c_lam = -RG_C * (jnp.maximum(-sp, 0.0) + jnp.log1p(jnp.exp(-jnp.abs(sp))))
    for h in range(RG_BLOCKS):
        lanes = slice(h * RG_BLOCK, (h + 1) * RG_BLOCK)
        cv = cv_ref[:, lanes]
        g = _dot(cv.astype(BF16), wg_ref[0, h])
        r = _sigmoid(g[:, :RG_BLOCK] + br_ref[0, :, lanes])
        gi = _sigmoid(g[:, RG_BLOCK:] + bi_ref[0, :, lanes])
        log_a = c_lam[:, lanes] * r
        a = jnp.exp(log_a)
        mult = jnp.sqrt(1.0 - a * a)
        a_ref[:, lanes] = a
        cv_ref[:, lanes] = mult * gi * cv

    @pl.when(i < NP_TILES)
    def _scan_prompt():
        for c in range(RG_WIDTH // RG_LC):
            lanes = slice(c * RG_LC, (c + 1) * RG_LC)

            def body(k, x, lanes=lanes):
                rows = pl.ds(pl.multiple_of(k * SUB, SUB), SUB)
                a = a_ref[rows, lanes]
                b = cv_ref[rows, lanes]
                y = a * _roll4(x) + b
                z = a * _roll4(y) + b
                h = jnp.where(lo, y, z)
                cv_ref[rows, lanes] = h
                return h

            hx_ref[:, lanes] = lax.fori_loop(0, TM // SUB, body, hx_ref[:, lanes])

    @pl.when(i == NP_TILES)
    def _scan_sample():
        for c in range(RG_WIDTH // RG_LC):
            lanes = slice(c * RG_LC, (c + 1) * RG_LC)

            def body(g, carry, lanes=lanes):
                r0 = pl.multiple_of(g * SUB, SUB)
                x = h0_ref[0, pl.ds(r0, SUB), lanes]
                for t in range(T_S):
                    rows = pl.ds(t * B_S + r0, SUB)
                    x = a_ref[rows, lanes] * x + cv_ref[rows, lanes]
                    cv_ref[rows, lanes] = x
                hs_ref[pl.ds(r0, SUB), lanes] = x
                return carry

            lax.fori_loop(0, B_S // SUB, body, 0)

    @pl.when(i == NP_TILES - 1)
    def _prompt_state():
        hp_ref[...] = hx_ref[...]

    rc = 64

    def out(j, carry):
        rows = pl.ds(pl.multiple_of(j * rc, rc), rc)
        z = zb_ref[rows, :]
        yb_ref[rows, :] = (cv_ref[rows, :] * (z * _sigmoid(z))).astype(BF16)
        return carry

    lax.fori_loop(0, TM // rc, out, 0)


def _rg_branch(proj, conv_w, conv_b, wg, b_r, b_i, lam, h0, conv0, l):
    const2 = lambda i: (0, 0)
    vec = _lspec(l, (1, RG_WIDTH))
    carry = pltpu.VMEM((SUB, RG_WIDTH), F32)
    return pl.pallas_call(
        _rg_kernel,
        grid=(N_TILES,),
        in_specs=[pl.BlockSpec((TM, RG_WIDTH), lambda i: (i, 1)),
                  pl.BlockSpec((TM, RG_WIDTH), lambda i: (i, 2)),
                  _lspec(l, (CONV_W, RG_WIDTH)),
                  vec,
                  _lspec(l, (RG_BLOCKS, RG_BLOCK, 2 * RG_BLOCK)),
                  vec, vec, vec,
                  _lspec(l, (B_S, RG_WIDTH)),
                  _lspec(l, ((CONV_W - 1) * B_S, RG_WIDTH))],
        out_specs=[pl.BlockSpec((TM, RG_WIDTH), lambda i: (i, 0)),
                   pl.BlockSpec((SUB, RG_WIDTH), const2),
                   pl.BlockSpec((B_S, RG_WIDTH), const2)],
        out_shape=[jax.ShapeDtypeStruct((N_ROWS, RG_WIDTH), BF16),
                   jax.ShapeDtypeStruct((SUB, RG_WIDTH), F32),
                   jax.ShapeDtypeStruct((B_S, RG_WIDTH), F32)],
        scratch_shapes=[pltpu.VMEM((TM, RG_WIDTH), F32),
                        pltpu.VMEM((TM, RG_WIDTH), F32),
                        carry, carry, carry, carry],
        compiler_params=_cparams(1),
        name="rg_branch",
    )(proj, proj, conv_w, conv_b, wg, b_r, b_i, lam, h0, conv0)


MERGE_TN = 512
GA_COL0 = (2 * S5_WIDTH + 2 * RG_WIDTH) // MERGE_TN
GB_COL0 = GA_COL0 + D_MODEL // MERGE_TN


def _merge_kernel(ya_ref, yb_ref, ga_ref, gb_ref, wa_ref, wb_ref, o_ref, wa_bf_ref, wb_bf_ref):
    @pl.when(pl.program_id(1) == 0)
    def _():
        wa_bf_ref[...] = wa_ref[0].astype(BF16)
        wb_bf_ref[...] = wb_ref[0].astype(BF16)

    pa = _dot(ya_ref[...], wa_bf_ref[...])
    pb = _dot(yb_ref[...], wb_bf_ref[...])
    o_ref[...] = (_sigmoid(ga_ref[...]) * pa + _sigmoid(gb_ref[...]) * pb).astype(BF16)


def _merge(ya, yb, proj, w_a, w_b, l):
    return pl.pallas_call(
        _merge_kernel,
        grid=(D_MODEL // MERGE_TN, N_TILES),
        in_specs=[pl.BlockSpec((TM, S5_WIDTH), lambda j, i: (i, 0)),
                  pl.BlockSpec((TM, RG_WIDTH), lambda j, i: (i, 0)),
                  pl.BlockSpec((TM, MERGE_TN), lambda j, i: (i, GA_COL0 + j)),
                  pl.BlockSpec((TM, MERGE_TN), lambda j, i: (i, GB_COL0 + j)),
                  pl.BlockSpec((1, S5_WIDTH, MERGE_TN), lambda j, i: (l, 0, j)),
                  pl.BlockSpec((1, RG_WIDTH, MERGE_TN), lambda j, i: (l, 0, j))],
        out_specs=pl.BlockSpec((TM, MERGE_TN), lambda j, i: (i, j)),
        out_shape=jax.ShapeDtypeStruct((N_ROWS, D_MODEL), BF16),
        scratch_shapes=[pltpu.VMEM((S5_WIDTH, MERGE_TN), BF16),
                        pltpu.VMEM((RG_WIDTH, MERGE_TN), BF16)],
        compiler_params=_cparams(2),
        name="merge",
    )(ya, yb, proj, proj, w_a, w_b)


OUT_TM = IO_TM


def _out_kernel(last, m_ref, x_ref, w_ref, gate_ref, g_ref, *rest):
    if last:
        yp_ref, ys_ref, wbf_ref, y_ref = rest
    else:
        sc_ref, sh_ref, xo_ref, xn_ref, wbf_ref = rest
    i = pl.program_id(0)

    @pl.when(i == 0)
    def _():
        wbf_ref[...] = w_ref[0].astype(BF16)

    out = _dot(m_ref[...], wbf_ref[...])
    for s in range(OUT_TM // MOD_ROWS):
        rows = slice(s * MOD_ROWS, (s + 1) * MOD_ROWS)
        x = x_ref[rows, :] + gate_ref[0, 0, 0] * out[rows, :]
        y = _rms(x, g_ref[0])
        if last:
            y_ref[rows, :] = y
        else:
            xo_ref[rows, :] = x
            xn_ref[rows, :] = (y * (1.0 + sc_ref[0, 0, 0]) + sh_ref[0, 0, 0]).astype(BF16)

    if last:
        @pl.when(i < IO_TILES_P)
        def _prompt():
            yp_ref[...] = pltpu.einshape("tbd->btd", y_ref[...].reshape(IO_TT, B_P, D_MODEL))

        @pl.when(i >= IO_TILES_P)
        def _sample():
            ys_ref[...] = y_ref[...]


def _out_layer(merged, x, w_out, mods, l, gain, gain_l):
    last = l + 1 == DEPTH
    row = pl.BlockSpec((OUT_TM, D_MODEL), lambda i: (i, 0))
    in_specs = [row, row,
                pl.BlockSpec((1, D_MODEL, D_MODEL), lambda i: (l, 0, 0),
                             pipeline_mode=pl.Buffered(1)),
                _mod_spec(l, MOD_GATE, OUT_TM),
                _lspec(gain_l, (1, D_MODEL))]
    args = [merged, x, w_out, mods, gain]
    scratch = [pltpu.VMEM((D_MODEL, D_MODEL), BF16)]
    if last:
        out_specs = [_prompt_io_spec, _sample_io_spec]
        out_shape = [jax.ShapeDtypeStruct((B_P, T_P, D_MODEL), F32),
                     jax.ShapeDtypeStruct((N_S, D_MODEL), F32)]
        scratch.append(pltpu.VMEM((OUT_TM, D_MODEL), F32))
    else:
        in_specs += [_mod_spec(l + 1, MOD_SCALE, OUT_TM), _mod_spec(l + 1, MOD_SHIFT, OUT_TM)]
        args += [mods, mods]
        out_specs = [row, row]
        out_shape = [jax.ShapeDtypeStruct((N_ROWS, D_MODEL), F32),
                     jax.ShapeDtypeStruct((N_ROWS, D_MODEL), BF16)]
    return pl.pallas_call(
        functools.partial(_out_kernel, last),
        grid=(N_ROWS // OUT_TM,),
        in_specs=in_specs,
        out_specs=out_specs,
        out_shape=out_shape,
        scratch_shapes=scratch,
        compiler_params=_cparams(1),
        name="out_last" if last else "out_layer",
    )(*args)


def _batch_major(rows, b, t):
    lead = rows.shape[:-2]
    return jnp.swapaxes(rows.reshape(lead + (t, b, rows.shape[-1])), -3, -2)


def _block_diag_in(bbt):
    gpc = S5_GROUPS // S5_CHUNKS
    w = bbt.reshape(DEPTH, S5_GROUP, S5_CHUNKS, gpc, S5_STATE)
    eye = jnp.eye(gpc, dtype=bbt.dtype)
    w = jnp.einsum("lkcgp,gh->lcgkhp", w, eye)
    return w.reshape(DEPTH, S5_CHUNKS, S5_CK, S5_CN)


def _block_diag_out(cmat):
    gpc = S5_GROUPS // S5_CHUNKS
    w = cmat.reshape(DEPTH, S5_CHUNKS, gpc, S5_GROUP, S5_STATE)
    eye = jnp.eye(gpc, dtype=cmat.dtype)
    w = jnp.einsum("lcgkp,gh->lchpgk", w, eye)
    return w.reshape(DEPTH, S5_CHUNKS, S5_CN, S5_CK)


def _mod_patterns(ada):
    a = ada.reshape(DEPTH, ada.shape[1], 3, D_MODEL).transpose(0, 2, 1, 3)
    p = jnp.tile(a[:, :, :B_P], (1, 1, MOD_ROWS // B_P, 1))
    s = a[:, :, B_P:B_P + B_S]
    return jnp.stack([p, s], axis=2)


def kernel(x_prompt, x_sample, state_s5_re, state_s5_im, state_rglru_h, state_conv,
           c_prompt, c_sample, w_ada, b_ada, norm_gain, w_in, b_in,
           s5_lam_re, s5_lam_im, s5_log_dt, s5_b_re, s5_b_im, s5_c_re, s5_c_im,
           s5_d, s5_w_glu, s5_b_glu, rg_conv_w, rg_conv_b, rg_w_r, rg_b_r, rg_w_i, rg_b_i,
           rg_lam, w_proj_a, w_proj_b, w_out, final_gain):
    sdt = state_s5_re.dtype

    to_lanes = lambda v: v.reshape(DEPTH, 1, S5_N)
    bt = lambda b: b.reshape(DEPTH, S5_N, S5_GROUP).transpose(0, 2, 1)
    log_dt = jnp.repeat(s5_log_dt, S5_STATE, axis=-1)
    a_re, a_im, bbt_re, bbt_im = _s5_prep(to_lanes(s5_lam_re), to_lanes(s5_lam_im),
                                          to_lanes(log_dt), bt(s5_b_re), bt(s5_b_im))
    wbu = jnp.concatenate([_block_diag_in(bbt_re), _block_diag_in(bbt_im)], axis=-2).astype(BF16)
    wc = jnp.concatenate([_block_diag_out(s5_c_re), _block_diag_out(s5_c_im)], axis=-1).astype(BF16)
    wg = jnp.concatenate([rg_w_r, rg_w_i], axis=-1).astype(BF16)

    n_c = B_P + B_S
    c_all = jnp.concatenate([c_prompt, c_sample, jnp.zeros((-n_c % SUB, D_MODEL), F32)], axis=0)
    mods = _mod_patterns(_ada(c_all, w_ada, b_ada))

    h0_re = state_s5_re.reshape(DEPTH, B_S, S5_N)
    h0_im = state_s5_im.reshape(DEPTH, B_S, S5_N)
    conv0 = state_conv.transpose(0, 2, 1, 3).reshape(DEPTH, (CONV_W - 1) * B_S, RG_WIDTH)
    gains = jnp.concatenate([norm_gain, final_gain[None]], axis=0).reshape(DEPTH + 1, 1, D_MODEL)

    xs_tm = x_sample.transpose(1, 0, 2).reshape(N_S, D_MODEL)
    x, xn = _embed(x_prompt, xs_tm, gains, mods)

    outs = {k: [] for k in ("s5re_p", "s5im_p", "h_p", "conv_p", "s5re_s", "s5im_s", "h_s", "conv_s")}
    y_prompt = y_sample = None
    half = SUB // 2
    xb_cols = slice(2 * S5_WIDTH, 2 * S5_WIDTH + RG_WIDTH)
    tail = CONV_W - 1
    for l in range(DEPTH):
        proj = _proj(xn, w_in, _lvec(b_in), l)
        ya, hp_s5, hs_re, hs_im = _s5_branch(
            proj, wbu, wc, a_re, a_im, _lvec(s5_d), s5_w_glu, _lvec(s5_b_glu), h0_re, h0_im, l)
        yb, hp, hs = _rg_branch(proj, rg_conv_w, _lvec(rg_conv_b), wg, _lvec(rg_b_r),
                                _lvec(rg_b_i), _lvec(rg_lam), state_rglru_h, conv0, l)
        merged = _merge(ya, yb, proj, w_proj_a, w_proj_b, l)
        res = _out_layer(merged, x, w_out, mods, l, gains, l + 1)
        if l + 1 < DEPTH:
            x, xn = res
        else:
            y_prompt, y_sample = res[0], _batch_major(res[1], B_S, T_S)

        outs["s5re_p"].append(hp_s5[:half])
        outs["s5im_p"].append(hp_s5[half:])
        outs["h_p"].append(hp[half:])
        outs["conv_p"].append(proj[N_P - tail * B_P:N_P, xb_cols])
        outs["s5re_s"].append(hs_re)
        outs["s5im_s"].append(hs_im)
        outs["h_s"].append(hs)
        outs["conv_s"].append(proj[N_ROWS - tail * B_S:, xb_cols])

    st = lambda k: jnp.stack(outs[k]).astype(sdt)
    s5_shape = lambda b: (DEPTH, b, S5_GROUPS, S5_STATE)
    return (y_prompt, y_sample,
            st("s5re_p").reshape(s5_shape(B_P)), st("s5im_p").reshape(s5_shape(B_P)),
            st("h_p"), _batch_major(st("conv_p"), B_P, tail),
            st("s5re_s").reshape(s5_shape(B_S)), st("s5im_s").reshape(s5_shape(B_S)),
            st("h_s"), _batch_major(st("conv_s"), B_S, tail))
```

```python
import functools

import jax
import jax.numpy as jnp
from jax import lax
from jax.experimental import pallas as pl
from jax.experimental.pallas import tpu as pltpu

F32 = jnp.float32
BF16 = jnp.bfloat16

D_MODEL = 2048
DEPTH = 4
B_P, T_P = 4, 2048
B_S, T_S = 128, 4
S5_WIDTH = 1024
S5_GROUP = 16
S5_GROUPS = 64
S5_STATE = 64
S5_N = S5_GROUPS * S5_STATE
RG_WIDTH = 2048
RG_BLOCKS = 16
RG_BLOCK = 128
RG_C = 8.0
CONV_W = 4
IN_COLS = 2 * S5_WIDTH + 2 * RG_WIDTH + 2 * D_MODEL
EPS = 1e-6
LOG2_E = 1.4426950408889634

TM = 512
N_P = B_P * T_P
N_S = B_S * T_S
N_ROWS = N_P + N_S
NP_TILES = N_P // TM
N_TILES = N_ROWS // TM
MOD_ROWS = 128
S5_CHUNKS = 8
S5_CK = S5_WIDTH // S5_CHUNKS
S5_CN = S5_N // S5_CHUNKS
SUB = 8
VMEM_LIMIT = 56 * 1024 * 1024

assert N_S == TM and N_P % TM == 0 and TM % MOD_ROWS == 0


def _cparams(n_axes, vmem=VMEM_LIMIT):
    return pltpu.CompilerParams(dimension_semantics=("arbitrary",) * n_axes,
                                vmem_limit_bytes=vmem)


def _dot(a, b):
    return jnp.dot(a, b, preferred_element_type=F32)


def _roll4(x):
    return pltpu.roll(x, SUB // 2, 0)


def _sigmoid(x):
    return 0.5 * jnp.tanh(0.5 * x) + 0.5


def _lvec(a):
    return a.reshape(DEPTH, 1, a.shape[-1])


def _lspec(l, shape, n_axes=1):
    zeros = (0,) * len(shape)
    if n_axes == 1:
        return pl.BlockSpec((1,) + tuple(shape), lambda i: (l,) + zeros)
    return pl.BlockSpec((1,) + tuple(shape), lambda j, i: (l,) + zeros)


def _s5_prep_kernel(lre_ref, lim_ref, ldt_ref, bre_ref, bim_ref,
                    ar_ref, ai_ref, bbre_ref, bbim_ref):
    lam_re = lre_ref[0]
    lam_im = lim_ref[0]
    dt = jnp.exp(ldt_ref[0])
    mag = jnp.exp(lam_re * dt)
    a_re = mag * jnp.cos(lam_im * dt)
    a_im = mag * jnp.sin(lam_im * dt)
    nr = a_re - 1.0
    ni = a_im
    den = lam_re * lam_re + lam_im * lam_im
    c_re = (nr * lam_re + ni * lam_im) / den
    c_im = (ni * lam_re - nr * lam_im) / den
    b_re = bre_ref[0]
    b_im = bim_ref[0]
    bbre_ref[0] = c_re * b_re - c_im * b_im
    bbim_ref[0] = c_re * b_im + c_im * b_re
    ar_ref[0] = a_re
    ai_ref[0] = a_im


def _s5_prep(lam_re, lam_im, log_dt, bt_re, bt_im):
    vec = pl.BlockSpec((1, 1, S5_N), lambda l: (l, 0, 0))
    mat = pl.BlockSpec((1, S5_GROUP, S5_N), lambda l: (l, 0, 0))
    return pl.pallas_call(
        _s5_prep_kernel,
        grid=(DEPTH,),
        in_specs=[vec, vec, vec, mat, mat],
        out_specs=[vec, vec, mat, mat],
        out_shape=[jax.ShapeDtypeStruct((DEPTH, 1, S5_N), F32)] * 2
        + [jax.ShapeDtypeStruct((DEPTH, S5_GROUP, S5_N), F32)] * 2,
        compiler_params=_cparams(1),
        name="s5_prep",
    )(lam_re, lam_im, log_dt, bt_re, bt_im)


ADA_TN = 1024


def _ada_kernel(c_ref, w_ref, b_ref, o_ref):
    o_ref[0] = _dot(c_ref[...].astype(BF16), w_ref[0].astype(BF16)) + b_ref[0]


def _ada(c_all, w_ada, b_ada):
    rows = c_all.shape[0]
    return pl.pallas_call(
        _ada_kernel,
        grid=(DEPTH, 3 * D_MODEL // ADA_TN),
        in_specs=[pl.BlockSpec((rows, D_MODEL), lambda l, j: (0, 0)),
                  pl.BlockSpec((1, D_MODEL, ADA_TN), lambda l, j: (l, 0, j)),
                  pl.BlockSpec((1, 1, ADA_TN), lambda l, j: (l, 0, j))],
        out_specs=pl.BlockSpec((1, rows, ADA_TN), lambda l, j: (l, 0, j)),
        out_shape=jax.ShapeDtypeStruct((DEPTH, rows, 3 * D_MODEL), F32),
        compiler_params=_cparams(2),
        name="ada",
    )(c_all, w_ada, _lvec(b_ada))


MOD_SHIFT, MOD_SCALE, MOD_GATE = 0, 1, 2


def _mod_spec(l, kind, tm):
    tiles_p = N_P // tm
    return pl.BlockSpec((1, 1, 1, MOD_ROWS, D_MODEL), lambda i: (l, kind, i // tiles_p, 0, 0))


def _rms(x, gain):
    ms = jnp.mean(x * x, axis=-1, keepdims=True)
    return x * lax.rsqrt(ms + EPS) * gain


IO_TM = TM
IO_TT = IO_TM // B_P
IO_TILES_P = N_P // IO_TM

_prompt_io_spec = pl.BlockSpec((B_P, IO_TT, D_MODEL), lambda i: (0, jnp.minimum(i, IO_TILES_P - 1), 0))
_sample_io_spec = pl.BlockSpec((IO_TM, D_MODEL), lambda i: (jnp.maximum(i - IO_TILES_P, 0), 0))


def _embed_kernel(xp_ref, xs_ref, g_ref, sc_ref, sh_ref, xo_ref, xn_ref):
    i = pl.program_id(0)

    @pl.when(i < IO_TILES_P)
    def _prompt():
        xo_ref[...] = pltpu.einshape("btd->tbd", xp_ref[...]).reshape(IO_TM, D_MODEL)

    @pl.when(i >= IO_TILES_P)
    def _sample():
        xo_ref[...] = xs_ref[...]

    for s in range(IO_TM // MOD_ROWS):
        rows = slice(s * MOD_ROWS, (s + 1) * MOD_ROWS)
        y = _rms(xo_ref[rows, :], g_ref[0])
        xn_ref[rows, :] = (y * (1.0 + sc_ref[0, 0, 0]) + sh_ref[0, 0, 0]).astype(BF16)


def _embed(x_prompt, x_sample, gain, mods):
    row = pl.BlockSpec((IO_TM, D_MODEL), lambda i: (i, 0))
    return pl.pallas_call(
        _embed_kernel,
        grid=(N_ROWS // IO_TM,),
        in_specs=[_prompt_io_spec, _sample_io_spec, _lspec(0, (1, D_MODEL)),
                  _mod_spec(0, MOD_SCALE, IO_TM), _mod_spec(0, MOD_SHIFT, IO_TM)],
        out_specs=[row, row],
        out_shape=[jax.ShapeDtypeStruct((N_ROWS, D_MODEL), F32),
                   jax.ShapeDtypeStruct((N_ROWS, D_MODEL), BF16)],
        compiler_params=_cparams(1),
        name="embed",
    )(x_prompt, x_sample, gain, mods, mods)


PROJ_TN = 1024
PROJ_TM = N_ROWS // 8
assert N_ROWS % PROJ_TM == 0 and PROJ_TM % (2 * SUB) == 0


def _proj_kernel(xn_ref, w_ref, b_ref, o_ref, wbf_ref):
    @pl.when(pl.program_id(1) == 0)
    def _():
        wbf_ref[...] = w_ref[0].astype(BF16)

    o_ref[...] = _dot(xn_ref[...], wbf_ref[...]) + b_ref[0]


def _proj(xn, w_in, b_in, l):
    return pl.pallas_call(
        _proj_kernel,
        grid=(IN_COLS // PROJ_TN, N_ROWS // PROJ_TM),
        in_specs=[pl.BlockSpec((PROJ_TM, D_MODEL), lambda j, i: (i, 0)),
                  pl.BlockSpec((1, D_MODEL, PROJ_TN), lambda j, i: (l, 0, j)),
                  pl.BlockSpec((1, 1, PROJ_TN), lambda j, i: (l, 0, j))],
        out_specs=pl.BlockSpec((PROJ_TM, PROJ_TN), lambda j, i: (i, j)),
        out_shape=jax.ShapeDtypeStruct((N_ROWS, IN_COLS), F32),
        scratch_shapes=[pltpu.VMEM((D_MODEL, PROJ_TN), BF16)],
        compiler_params=_cparams(2),
        name="proj_in",
    )(xn, w_in, b_in)


def _gelu(y):
    return 0.5 * y * (1.0 + lax.erf(y * (2.0 ** -0.5)))


def _s5_kernel(u_ref, z_ref, wbu_ref, wc_ref, ar_ref, ai_ref, d_ref,
               wglu_ref, bglu_ref, h0re_ref, h0im_ref,
               ya_ref, hp_ref, hsre_ref, hsim_ref,
               bu_ref, y_ref, g_ref, p_ref, wglu_bf_ref):
    i = pl.program_id(0)
    half = SUB // 2

    @pl.when(i == 0)
    def _init():
        wglu_bf_ref[...] = wglu_ref[0].astype(BF16)
        p_ref[...] = jnp.zeros_like(p_ref)

    def abar(c):
        lanes = slice(c * S5_CN, (c + 1) * S5_CN)
        a_re = jnp.broadcast_to(ar_ref[0, :, lanes], (SUB, S5_CN))
        a_im = jnp.broadcast_to(ai_ref[0, :, lanes], (SUB, S5_CN))
        return a_re, a_im

    @pl.when(i < NP_TILES)
    def _prompt():
        groups = TM // SUB
        lo3 = lax.broadcasted_iota(jnp.int32, (groups, SUB, S5_CK), 1) < half
        lo = lax.broadcasted_iota(jnp.int32, (SUB, S5_CN), 0) < half
        for c in range(S5_CHUNKS):
            lanes = slice(c * S5_CN, (c + 1) * S5_CN)
            chans = slice(c * S5_CK, (c + 1) * S5_CK)
            buf = bu_ref.at[c % 2]
            v = u_ref[:, chans].reshape(groups, SUB, S5_CK)
            r = pltpu.roll(v, half, 1)
            zero = jnp.zeros_like(v)
            t_even = jnp.concatenate([jnp.where(lo3, v, zero), jnp.where(lo3, zero, r)], axis=-1)
            t_odd = jnp.concatenate([jnp.where(lo3, r, zero), jnp.where(lo3, zero, v)], axis=-1)
            u2 = jnp.stack([t_even, t_odd], axis=1).reshape(2 * TM, 2 * S5_CK)
            buf[...] = _dot(u2.astype(BF16), wbu_ref[0, c])

            a_re, a_im = abar(c)
            a_sw = jnp.where(lo, -a_im, a_im)
            p = p_ref[:, lanes]
            for t in range(TM // B_P):
                rows = slice(t * SUB, (t + 1) * SUB)
                p = a_re * p + a_sw * _roll4(p) + buf[rows, :]
                buf[rows, :] = p
            p_ref[:, lanes] = p

            q = _dot(buf[...].astype(BF16), wc_ref[0, c])
            q = q.reshape(2 * groups, SUB, 2 * S5_CK)
            dd = q[:, :, :S5_CK] - pltpu.roll(q[:, :, S5_CK:], half, 1)
            dd = dd.reshape(groups, 2, SUB, S5_CK)
            yv = jnp.where(lo3, dd[:, 0], pltpu.roll(dd[:, 1], half, 1))
            y_ref[:, chans] = yv.reshape(TM, S5_CK)

    @pl.when(i == NP_TILES)
    def _sample():
        for c in range(S5_CHUNKS):
            lanes = slice(c * S5_CN, (c + 1) * S5_CN)
            chans = slice(c * S5_CK, (c + 1) * S5_CK)
            buf = bu_ref.at[c % 2]
            ub = u_ref[:, chans].astype(BF16)
            buf[0:TM, :] = _dot(ub, wbu_ref[0, c, 0:S5_CK, :])
            buf[TM:2 * TM, :] = _dot(ub, wbu_ref[0, c, S5_CK:2 * S5_CK, :])
            a_re, a_im = abar(c)

            def body(g, carry, a_re=a_re, a_im=a_im, lanes=lanes, buf=buf):
                r0 = pl.multiple_of(g * SUB, SUB)
                xr = h0re_ref[0, pl.ds(r0, SUB), lanes]
                xi = h0im_ref[0, pl.ds(r0, SUB), lanes]
                for t in range(T_S):
                    rre = pl.ds(t * B_S + r0, SUB)
                    rim = pl.ds(TM + t * B_S + r0, SUB)
                    xr, xi = (a_re * xr - a_im * xi + buf[rre, :], a_re * xi + a_im * xr + buf[rim, :])
                    buf[rre, :] = xr
                    buf[rim, :] = xi
                hsre_ref[pl.ds(r0, SUB), lanes] = xr
                hsim_ref[pl.ds(r0, SUB), lanes] = xi
                return carry

            lax.fori_loop(0, B_S // SUB, body, 0)
            y_ref[:, chans] = (_dot(buf[0:TM, :].astype(BF16), wc_ref[0, c, :, 0:S5_CK])
                               - _dot(buf[TM:2 * TM, :].astype(BF16), wc_ref[0, c, :, S5_CK:2 * S5_CK]))

    @pl.when(i == NP_TILES - 1)
    def _prompt_state():
        hp_ref[...] = p_ref[...]

    rc = 64

    def act(j, carry):
        rows = pl.ds(pl.multiple_of(j * rc, rc), rc)
        y_ref[rows, :] = _gelu(y_ref[rows, :] + d_ref[0] * u_ref[rows, :])
        return carry

    lax.fori_loop(0, TM // rc, act, 0)
    g_ref[...] = _dot(y_ref[...].astype(BF16), wglu_bf_ref[...])

    def gate(j, carry):
        rows = pl.ds(pl.multiple_of(j * rc, rc), rc)
        y = y_ref[rows, :]
        y = y * _sigmoid(g_ref[rows, :] + bglu_ref[0])
        z = z_ref[rows, :]
        ya_ref[rows, :] = (y * (z * _sigmoid(z))).astype(BF16)
        return carry

    lax.fori_loop(0, TM // rc, gate, 0)


def _s5_branch(proj, wbu, wc, a_re, a_im, d, w_glu, b_glu, h0_re, h0_im, l):
    const2 = lambda i: (0, 0)
    state_s = pl.BlockSpec((B_S, S5_N), const2)
    once = dict(pipeline_mode=pl.Buffered(1))
    lconst = lambda shape: pl.BlockSpec((1,) + shape, lambda i: (l,) + (0,) * len(shape), **once)
    return pl.pallas_call(
        _s5_kernel,
        grid=(N_TILES,),
        in_specs=[pl.BlockSpec((TM, S5_WIDTH), lambda i: (i, 0)),
                  pl.BlockSpec((TM, S5_WIDTH), lambda i: (i, 1)),
                  lconst((S5_CHUNKS, 2 * S5_CK, S5_CN)),
                  lconst((S5_CHUNKS, S5_CN, 2 * S5_CK)),
                  _lspec(l, (1, S5_N)),
                  _lspec(l, (1, S5_N)),
                  _lspec(l, (1, S5_WIDTH)),
                  lconst((S5_WIDTH, S5_WIDTH)),
                  _lspec(l, (1, S5_WIDTH)),
                  lconst((B_S, S5_N)),
                  lconst((B_S, S5_N))],
        out_specs=[pl.BlockSpec((TM, S5_WIDTH), lambda i: (i, 0)),
                   pl.BlockSpec((SUB, S5_N), const2), state_s, state_s],
        out_shape=[jax.ShapeDtypeStruct((N_ROWS, S5_WIDTH), BF16),
                   jax.ShapeDtypeStruct((SUB, S5_N), F32),
                   jax.ShapeDtypeStruct((B_S, S5_N), F32),
                   jax.ShapeDtypeStruct((B_S, S5_N), F32)],
        scratch_shapes=[pltpu.VMEM((2, 2 * TM, S5_CN), F32),
                        pltpu.VMEM((TM, S5_WIDTH), F32),
                        pltpu.VMEM((TM, S5_WIDTH), F32),
                        pltpu.VMEM((SUB, S5_N), F32),
                        pltpu.VMEM((S5_WIDTH, S5_WIDTH), BF16)],
        compiler_params=_cparams(1),
        name="s5_branch",
    )(proj, proj, wbu, wc, a_re, a_im, d, w_glu, b_glu, h0_re, h0_im)


RG_LC = 512


RGM_TM = 256
RGM_TILES = N_P // RGM_TM
MERGE_TN = 512
XB_COL, ZB_COL, GA_COL, GB_COL = 1, 2, 3, 4
assert RG_WIDTH == D_MODEL and 2 * S5_WIDTH == RG_WIDTH


def _conv_taps(cw_ref, cb_ref, lanes):
    taps = [jnp.broadcast_to(cw_ref[0, k:k + 1, lanes], (SUB, RG_LC)) for k in range(CONV_W)]
    return taps, jnp.broadcast_to(cb_ref[0, :, lanes], (SUB, RG_LC))


def _rg_gates(cv_ref, a_ref, wg_ref, br_ref, bi_ref, lam_ref):
    sp = lam_ref[0]
    c_lam = -RG_C * (jnp.maximum(-sp, 0.0) + jnp.log1p(jnp.exp(-jnp.abs(sp))))
    k_lam = c_lam * (0.5 * LOG2_E)
    for h in range(RG_BLOCKS):
        lanes = slice(h * RG_BLOCK, (h + 1) * RG_BLOCK)
        cv = cv_ref[:, lanes]
        g = _dot(cv.astype(BF16), wg_ref[0, h])
        k = k_lam[:, lanes]
        a = jnp.exp2(k * jnp.tanh(0.5 * (g[:, :RG_BLOCK] + br_ref[0, :, lanes])) + k)
        gi = _sigmoid(g[:, RG_BLOCK:] + bi_ref[0, :, lanes])
        m2 = 1.0 - a * a
        mult = jnp.where(m2 > 0.0, m2 * lax.rsqrt(m2), 0.0)
        a_ref[:, lanes] = a
        cv_ref[:, lanes] = mult * gi * cv


def _merge_tile(ya_ref, yb_ref, ga_ref, gb_ref, wa_ref, wb_ref, o_ref):
    for j in range(D_MODEL // MERGE_TN):
        cols = slice(j * MERGE_TN, (j + 1) * MERGE_TN)
        pa = _dot(ya_ref[...], wa_ref[0, :, cols])
        pb = _dot(yb_ref[...], wb_ref[0, :, cols])
        o_ref[:, cols] = (_sigmoid(ga_ref[:, cols]) * pa + _sigmoid(gb_ref[:, cols]) * pb).astype(BF16)


def _rgm_prompt_kernel(xb_ref, zb_ref, ya_ref, ga_ref, gb_ref, cw_ref, cb_ref, wg_ref, br_ref,
                       bi_ref, lam_ref, wa_ref, wb_ref,
                       m_ref, hp_ref,
                       cv_ref, a_ref, yb_cur_ref, yb_prev_ref, hx_ref, pv_ref, pr_ref, ps_ref):
    s = pl.program_id(0)
    groups = RGM_TM // SUB

    @pl.when(s == 0)
    def _init():
        for r in (hx_ref, pv_ref, pr_ref, ps_ref, yb_prev_ref):
            r[...] = jnp.zeros_like(r)

    _merge_tile(ya_ref, yb_prev_ref, ga_ref, gb_ref, wa_ref, wb_ref, m_ref)

    lo = lax.broadcasted_iota(jnp.int32, (SUB, RG_LC), 0) < SUB // 2
    for c in range(RG_WIDTH // RG_LC):
        lanes = slice(c * RG_LC, (c + 1) * RG_LC)
        (w0, w1, w2, w3), cb = _conv_taps(cw_ref, cb_ref, lanes)
        v_prev, r_prev, s_prev = pv_ref[:, lanes], pr_ref[:, lanes], ps_ref[:, lanes]
        for k in range(groups):
            rows = slice(k * SUB, (k + 1) * SUB)
            v = xb_ref[rows, lanes]
            r = _roll4(v)
            sh = jnp.where(lo, r_prev, r)
            cv_ref[rows, lanes] = cb + w3 * v + w2 * sh + w1 * v_prev + w0 * s_prev
            v_prev, r_prev, s_prev = v, r, sh
        pv_ref[:, lanes], pr_ref[:, lanes], ps_ref[:, lanes] = v_prev, r_prev, s_prev

    _rg_gates(cv_ref, a_ref, wg_ref, br_ref, bi_ref, lam_ref)

    for c in range(RG_WIDTH // RG_LC):
        lanes = slice(c * RG_LC, (c + 1) * RG_LC)
        x = hx_ref[:, lanes]
        for k in range(groups):
            rows = slice(k * SUB, (k + 1) * SUB)
            a = a_ref[rows, lanes]
            b = cv_ref[rows, lanes]
            y = a * _roll4(x) + b
            z = a * _roll4(y) + b
            x = jnp.where(lo, y, z)
            cv_ref[rows, lanes] = x
        hx_ref[:, lanes] = x

    @pl.when(s == RGM_TILES - 1)
    def _prompt_state():
        hp_ref[...] = hx_ref[...]

    rc = 64
    for j in range(RGM_TM // rc):
        rows = slice(j * rc, (j + 1) * rc)
        z = zb_ref[rows, :]
        yb_cur_ref[rows, :] = (cv_ref[rows, :] * (z * _sigmoid(z))).astype(BF16)
    yb_prev_ref[...] = yb_cur_ref[...]


def _rgm_prompt(proj, ya, conv_w, conv_b, wg, b_r, b_i, lam, w_a, w_b, l):
    vec = _lspec(l, (1, RG_WIDTH))
    cur = lambda col: pl.BlockSpec((RGM_TM, RG_WIDTH), lambda s: (jnp.minimum(s, RGM_TILES - 1), col))
    prev = lambda w, col: pl.BlockSpec((RGM_TM, w), lambda s: (jnp.maximum(s - 1, 0), col))
    once = dict(pipeline_mode=pl.Buffered(1))
    carry = pltpu.VMEM((SUB, RG_WIDTH), F32)
    return pl.pallas_call(
        _rgm_prompt_kernel,
        grid=(RGM_TILES + 1,),
        in_specs=[cur(XB_COL), cur(ZB_COL), prev(S5_WIDTH, 0), prev(D_MODEL, GA_COL), prev(D_MODEL, GB_COL),
                  _lspec(l, (CONV_W, RG_WIDTH)), vec,
                  _lspec(l, (RG_BLOCKS, RG_BLOCK, 2 * RG_BLOCK)), vec, vec, vec,
                  pl.BlockSpec((1, S5_WIDTH, D_MODEL), lambda s: (l, 0, 0), **once),
                  pl.BlockSpec((1, RG_WIDTH, D_MODEL), lambda s: (l, 0, 0), **once)],
        out_specs=[prev(D_MODEL, 0), pl.BlockSpec((SUB, RG_WIDTH), lambda s: (0, 0))],
        out_shape=[jax.ShapeDtypeStruct((N_ROWS, D_MODEL), BF16),
                   jax.ShapeDtypeStruct((SUB, RG_WIDTH), F32)],
        scratch_shapes=[pltpu.VMEM((RGM_TM, RG_WIDTH), F32),
                        pltpu.VMEM((RGM_TM, RG_WIDTH), F32),
                        pltpu.VMEM((RGM_TM, RG_WIDTH), BF16),
                        pltpu.VMEM((RGM_TM, RG_WIDTH), BF16),
                        carry, carry, carry, carry],
        compiler_params=_cparams(1),
        name="rgm_prompt",
    )(proj, proj, ya, proj, proj, conv_w, conv_b, wg, b_r, b_i, lam, w_a, w_b)


def _rgm_sample_kernel(merged_hbm_ref, xb_ref, zb_ref, ya_ref, ga_ref, gb_ref, cw_ref, cb_ref, wg_ref,
                       br_ref, bi_ref, lam_ref, wa_ref, wb_ref, h0_ref, c0_ref,
                       m_ref, hs_ref,
                       cv_ref, a_ref, yb_ref):
    del merged_hbm_ref

    def src(t, r0, lanes):
        if t >= 0:
            return xb_ref[pl.ds(t * B_S + r0, SUB), lanes]
        return c0_ref[0, pl.ds((CONV_W - 1 + t) * B_S + r0, SUB), lanes]

    for c in range(RG_WIDTH // RG_LC):
        lanes = slice(c * RG_LC, (c + 1) * RG_LC)
        (w0, w1, w2, w3), cb = _conv_taps(cw_ref, cb_ref, lanes)

        def body(g, carry, w0=w0, w1=w1, w2=w2, w3=w3, cb=cb, lanes=lanes):
            r0 = pl.multiple_of(g * SUB, SUB)
            for t in range(T_S):
                cv_ref[pl.ds(t * B_S + r0, SUB), lanes] = (
                    cb + w3 * src(t, r0, lanes) + w2 * src(t - 1, r0, lanes)
                    + w1 * src(t - 2, r0, lanes) + w0 * src(t - 3, r0, lanes))
            return carry

        lax.fori_loop(0, B_S // SUB, body, 0)

    _rg_gates(cv_ref, a_ref, wg_ref, br_ref, bi_ref, lam_ref)

    for c in range(RG_WIDTH // RG_LC):
        lanes = slice(c * RG_LC, (c + 1) * RG_LC)

        def body(g, carry, lanes=lanes):
            r0 = pl.multiple_of(g * SUB, SUB)
            x = h0_ref[0, pl.ds(r0, SUB), lanes]
            for t in range(T_S):
                rows = pl.ds(t * B_S + r0, SUB)
                x = a_ref[rows, lanes] * x + cv_ref[rows, lanes]
                cv_ref[rows, lanes] = x
            hs_ref[pl.ds(r0, SUB), lanes] = x
            return carry

        lax.fori_loop(0, B_S // SUB, body, 0)

    rc = 64

    def out(j, carry):
        rows = pl.ds(pl.multiple_of(j * rc, rc), rc)
        z = zb_ref[rows, :]
        yb_ref[rows, :] = (cv_ref[rows, :] * (z * _sigmoid(z))).astype(BF16)
        return carry

    lax.fori_loop(0, N_S // rc, out, 0)
    _merge_tile(ya_ref, yb_ref, ga_ref, gb_ref, wa_ref, wb_ref, m_ref)


def _rgm_sample(merged, proj, ya, conv_w, conv_b, wg, b_r, b_i, lam, w_a, w_b, h0, conv0, l):
    once = dict(pipeline_mode=pl.Buffered(1))
    tile = N_P // N_S
    rows = lambda w, col: pl.BlockSpec((N_S, w), lambda i: (tile, col), **once)
    lconst = lambda shape: pl.BlockSpec((1,) + shape, lambda i: (l,) + (0,) * len(shape), **once)
    vec = lconst((1, RG_WIDTH))
    return pl.pallas_call(
        _rgm_sample_kernel,
        grid=(1,),
        in_specs=[pl.BlockSpec(memory_space=pl.ANY),
                  rows(RG_WIDTH, XB_COL), rows(RG_WIDTH, ZB_COL), rows(S5_WIDTH, 0),
                  rows(D_MODEL, GA_COL), rows(D_MODEL, GB_COL),
                  lconst((CONV_W, RG_WIDTH)), vec,
                  lconst((RG_BLOCKS, RG_BLOCK, 2 * RG_BLOCK)), vec, vec, vec,
                  lconst((S5_WIDTH, D_MODEL)), lconst((RG_WIDTH, D_MODEL)),
                  lconst((B_S, RG_WIDTH)), lconst(((CONV_W - 1) * B_S, RG_WIDTH))],
        out_specs=[pl.BlockSpec((N_S, D_MODEL), lambda i: (tile, 0)),
                   pl.BlockSpec((B_S, RG_WIDTH), lambda i: (0, 0))],
        out_shape=[jax.ShapeDtypeStruct((N_ROWS, D_MODEL), BF16),
                   jax.ShapeDtypeStruct((B_S, RG_WIDTH), F32)],
        input_output_aliases={0: 0},
        scratch_shapes=[pltpu.VMEM((N_S, RG_WIDTH), F32),
                        pltpu.VMEM((N_S, RG_WIDTH), F32),
                        pltpu.VMEM((N_S, RG_WIDTH), BF16)],
        compiler_params=_cparams(1),
        name="rgm_sample",
    )(merged, proj, proj, ya, proj, proj, conv_w, conv_b, wg, b_r, b_i, lam, w_a, w_b, h0, conv0)


OUT_TM = IO_TM


def _out_kernel(last, m_ref, x_ref, w_ref, gate_ref, g_ref, *rest):
    if last:
        yp_ref, ys_ref, y_ref = rest
    else:
        sc_ref, sh_ref, xo_ref, xn_ref = rest
    i = pl.program_id(0)

    out = _dot(m_ref[...], w_ref[0])
    for s in range(OUT_TM // MOD_ROWS):
        rows = slice(s * MOD_ROWS, (s + 1) * MOD_ROWS)
        x = x_ref[rows, :] + gate_ref[0, 0, 0] * out[rows, :]
        y = _rms(x, g_ref[0])
        if last:
            y_ref[rows, :] = y
        else:
            xo_ref[rows, :] = x
            xn_ref[rows, :] = (y * (1.0 + sc_ref[0, 0, 0]) + sh_ref[0, 0, 0]).astype(BF16)

    if last:
        @pl.when(i < IO_TILES_P)
        def _prompt():
            yp_ref[...] = pltpu.einshape("tbd->btd", y_ref[...].reshape(IO_TT, B_P, D_MODEL))

        @pl.when(i >= IO_TILES_P)
        def _sample():
            ys_ref[...] = y_ref[...]


def _out_layer(merged, x, w_out, mods, l, gain, gain_l):
    last = l + 1 == DEPTH
    row = pl.BlockSpec((OUT_TM, D_MODEL), lambda i: (i, 0))
    in_specs = [row, row,
                pl.BlockSpec((1, D_MODEL, D_MODEL), lambda i: (l, 0, 0),
                             pipeline_mode=pl.Buffered(1)),
                _mod_spec(l, MOD_GATE, OUT_TM),
                _lspec(gain_l, (1, D_MODEL))]
    args = [merged, x, w_out, mods, gain]
    scratch = []
    if last:
        out_specs = [_prompt_io_spec, _sample_io_spec]
        out_shape = [jax.ShapeDtypeStruct((B_P, T_P, D_MODEL), F32),
                     jax.ShapeDtypeStruct((N_S, D_MODEL), F32)]
        scratch.append(pltpu.VMEM((OUT_TM, D_MODEL), F32))
    else:
        in_specs += [_mod_spec(l + 1, MOD_SCALE, OUT_TM), _mod_spec(l + 1, MOD_SHIFT, OUT_TM)]
        args += [mods, mods]
        out_specs = [row, row]
        out_shape = [jax.ShapeDtypeStruct((N_ROWS, D_MODEL), F32),
                     jax.ShapeDtypeStruct((N_ROWS, D_MODEL), BF16)]
    return pl.pallas_call(
        functools.partial(_out_kernel, last),
        grid=(N_ROWS // OUT_TM,),
        in_specs=in_specs,
        out_specs=out_specs,
        out_shape=out_shape,
        scratch_shapes=scratch,
        compiler_params=_cparams(1),
        name="out_last" if last else "out_layer",
    )(*args)


def _batch_major(rows, b, t):
    lead = rows.shape[:-2]
    return jnp.swapaxes(rows.reshape(lead + (t, b, rows.shape[-1])), -3, -2)


def _block_diag_in(bbt):
    gpc = S5_GROUPS // S5_CHUNKS
    w = bbt.reshape(DEPTH, S5_GROUP, S5_CHUNKS, gpc, S5_STATE)
    eye = jnp.eye(gpc, dtype=bbt.dtype)
    w = jnp.einsum("lkcgp,gh->lcgkhp", w, eye)
    return w.reshape(DEPTH, S5_CHUNKS, S5_CK, S5_CN)


def _block_diag_out(cmat):
    gpc = S5_GROUPS // S5_CHUNKS
    w = cmat.reshape(DEPTH, S5_CHUNKS, gpc, S5_GROUP, S5_STATE)
    eye = jnp.eye(gpc, dtype=cmat.dtype)
    w = jnp.einsum("lcgkp,gh->lchpgk", w, eye)
    return w.reshape(DEPTH, S5_CHUNKS, S5_CN, S5_CK)


def _mod_patterns(ada):
    a = ada.reshape(DEPTH, ada.shape[1], 3, D_MODEL).transpose(0, 2, 1, 3)
    p = jnp.tile(a[:, :, :B_P], (1, 1, MOD_ROWS // B_P, 1))
    s = a[:, :, B_P:B_P + B_S]
    return jnp.stack([p, s], axis=2)


def kernel(x_prompt, x_sample, state_s5_re, state_s5_im, state_rglru_h, state_conv,
           c_prompt, c_sample, w_ada, b_ada, norm_gain, w_in, b_in,
           s5_lam_re, s5_lam_im, s5_log_dt, s5_b_re, s5_b_im, s5_c_re, s5_c_im,
           s5_d, s5_w_glu, s5_b_glu, rg_conv_w, rg_conv_b, rg_w_r, rg_b_r, rg_w_i, rg_b_i,
           rg_lam, w_proj_a, w_proj_b, w_out, final_gain):
    sdt = state_s5_re.dtype

    to_lanes = lambda v: v.reshape(DEPTH, 1, S5_N)
    bt = lambda b: b.reshape(DEPTH, S5_N, S5_GROUP).transpose(0, 2, 1)
    log_dt = jnp.repeat(s5_log_dt, S5_STATE, axis=-1)
    a_re, a_im, bbt_re, bbt_im = _s5_prep(to_lanes(s5_lam_re), to_lanes(s5_lam_im),
                                          to_lanes(log_dt), bt(s5_b_re), bt(s5_b_im))
    wbu = jnp.concatenate([_block_diag_in(bbt_re), _block_diag_in(bbt_im)], axis=-2).astype(BF16)
    wc = jnp.concatenate([_block_diag_out(s5_c_re), _block_diag_out(s5_c_im)], axis=-1).astype(BF16)
    wg = jnp.concatenate([rg_w_r, rg_w_i], axis=-1).astype(BF16)
    wa_bf, wb_bf, wo_bf = w_proj_a.astype(BF16), w_proj_b.astype(BF16), w_out.astype(BF16)

    n_c = B_P + B_S
    c_all = jnp.concatenate([c_prompt, c_sample, jnp.zeros((-n_c % SUB, D_MODEL), F32)], axis=0)
    mods = _mod_patterns(_ada(c_all, w_ada, b_ada))

    h0_re = state_s5_re.reshape(DEPTH, B_S, S5_N)
    h0_im = state_s5_im.reshape(DEPTH, B_S, S5_N)
    conv0 = state_conv.transpose(0, 2, 1, 3).reshape(DEPTH, (CONV_W - 1) * B_S, RG_WIDTH)
    gains = jnp.concatenate([norm_gain, final_gain[None]], axis=0).reshape(DEPTH + 1, 1, D_MODEL)

    xs_tm = x_sample.transpose(1, 0, 2).reshape(N_S, D_MODEL)
    x, xn = _embed(x_prompt, xs_tm, gains, mods)

    outs = {k: [] for k in ("s5re_p", "s5im_p", "h_p", "conv_p", "s5re_s", "s5im_s", "h_s", "conv_s")}
    y_prompt = y_sample = None
    half = SUB // 2
    xb_cols = slice(2 * S5_WIDTH, 2 * S5_WIDTH + RG_WIDTH)
    tail = CONV_W - 1
    for l in range(DEPTH):
        proj = _proj(xn, w_in, _lvec(b_in), l)
        ya, hp_s5, hs_re, hs_im = _s5_branch(
            proj, wbu, wc, a_re, a_im, _lvec(s5_d), s5_w_glu, _lvec(s5_b_glu), h0_re, h0_im, l)
        rg_params = (rg_conv_w, _lvec(rg_conv_b), wg, _lvec(rg_b_r), _lvec(rg_b_i), _lvec(rg_lam),
                     wa_bf, wb_bf)
        merged, hp = _rgm_prompt(proj, ya, *rg_params, l)
        merged, hs = _rgm_sample(merged, proj, ya, *rg_params, state_rglru_h, conv0, l)
        res = _out_layer(merged, x, wo_bf, mods, l, gains, l + 1)
        if l + 1 < DEPTH:
            x, xn = res
        else:
            y_prompt, y_sample = res[0], _batch_major(res[1], B_S, T_S)

        outs["s5re_p"].append(hp_s5[:half])
        outs["s5im_p"].append(hp_s5[half:])
        outs["h_p"].append(hp[half:])
        outs["conv_p"].append(proj[N_P - tail * B_P:N_P, xb_cols])
        outs["s5re_s"].append(hs_re)
        outs["s5im_s"].append(hs_im)
        outs["h_s"].append(hs)
        outs["conv_s"].append(proj[N_ROWS - tail * B_S:, xb_cols])

    st = lambda k: jnp.stack(outs[k]).astype(sdt)
    s5_shape = lambda b: (DEPTH, b, S5_GROUPS, S5_STATE)
    return (y_prompt, y_sample,
            st("s5re_p").reshape(s5_shape(B_P)), st("s5im_p").reshape(s5_shape(B_P)),
            st("h_p"), _batch_major(st("conv_p"), B_P, tail),
            st("s5re_s").reshape(s5_shape(B_S)), st("s5im_s").reshape(s5_shape(B_S)),
            st("h_s"), _batch_major(st("conv_s"), B_S, tail))
```

```python
import functools

import jax
import jax.numpy as jnp
from jax import lax
from jax.experimental import pallas as pl
from jax.experimental.pallas import tpu as pltpu

F32 = jnp.float32
BF16 = jnp.bfloat16

D_MODEL = 2048
DEPTH = 4
B_P, T_P = 4, 2048
B_S, T_S = 128, 4
S5_WIDTH = 1024
S5_GROUP = 16
S5_GROUPS = 64
S5_STATE = 64
S5_N = S5_GROUPS * S5_STATE
RG_WIDTH = 2048
RG_BLOCKS = 16
RG_BLOCK = 128
RG_C = 8.0
CONV_W = 4
IN_COLS = 2 * S5_WIDTH + 2 * RG_WIDTH + 2 * D_MODEL
EPS = 1e-6
LOG2_E = 1.4426950408889634

TM = 512
N_P = B_P * T_P
N_S = B_S * T_S
N_ROWS = N_P + N_S
NP_TILES = N_P // TM
N_TILES = N_ROWS // TM
MOD_ROWS = 128
S5_CHUNKS = 8
S5_CK = S5_WIDTH // S5_CHUNKS
S5_CN = S5_N // S5_CHUNKS
SUB = 8
VMEM_LIMIT = 56 * 1024 * 1024

assert N_S == TM and N_P % TM == 0 and TM % MOD_ROWS == 0


def _cparams(n_axes, vmem=VMEM_LIMIT):
    return pltpu.CompilerParams(dimension_semantics=("arbitrary",) * n_axes,
                                vmem_limit_bytes=vmem)


def _dot(a, b):
    return jnp.dot(a, b, preferred_element_type=F32)


def _roll4(x):
    return pltpu.roll(x, SUB // 2, 0)


def _sigmoid(x):
    return 0.5 * jnp.tanh(0.5 * x) + 0.5


def _sigmoid2(x):
    return jnp.tanh(0.5 * x) + 1.0


def _lvec(a):
    return a.reshape(DEPTH, 1, a.shape[-1])


def _lspec(l, shape, n_axes=1):
    zeros = (0,) * len(shape)
    if n_axes == 1:
        return pl.BlockSpec((1,) + tuple(shape), lambda i: (l,) + zeros)
    return pl.BlockSpec((1,) + tuple(shape), lambda j, i: (l,) + zeros)


def _s5_prep_kernel(lre_ref, lim_ref, ldt_ref, bre_ref, bim_ref,
                    ar_ref, ai_ref, bbre_ref, bbim_ref):
    lam_re = lre_ref[0]
    lam_im = lim_ref[0]
    dt = jnp.exp(ldt_ref[0])
    mag = jnp.exp(lam_re * dt)
    a_re = mag * jnp.cos(lam_im * dt)
    a_im = mag * jnp.sin(lam_im * dt)
    nr = a_re - 1.0
    ni = a_im
    den = lam_re * lam_re + lam_im * lam_im
    c_re = (nr * lam_re + ni * lam_im) / den
    c_im = (ni * lam_re - nr * lam_im) / den
    b_re = bre_ref[0]
    b_im = bim_ref[0]
    bbre_ref[0] = c_re * b_re - c_im * b_im
    bbim_ref[0] = c_re * b_im + c_im * b_re
    ar_ref[0] = a_re
    ai_ref[0] = a_im


def _s5_prep(lam_re, lam_im, log_dt, bt_re, bt_im):
    vec = pl.BlockSpec((1, 1, S5_N), lambda l: (l, 0, 0))
    mat = pl.BlockSpec((1, S5_GROUP, S5_N), lambda l: (l, 0, 0))
    return pl.pallas_call(
        _s5_prep_kernel,
        grid=(DEPTH,),
        in_specs=[vec, vec, vec, mat, mat],
        out_specs=[vec, vec, mat, mat],
        out_shape=[jax.ShapeDtypeStruct((DEPTH, 1, S5_N), F32)] * 2
        + [jax.ShapeDtypeStruct((DEPTH, S5_GROUP, S5_N), F32)] * 2,
        compiler_params=_cparams(1),
        name="s5_prep",
    )(lam_re, lam_im, log_dt, bt_re, bt_im)


ADA_TN = 1024


def _ada_kernel(c_ref, w_ref, b_ref, o_ref):
    o_ref[0] = _dot(c_ref[...].astype(BF16), w_ref[0].astype(BF16)) + b_ref[0]


def _ada(c_all, w_ada, b_ada):
    rows = c_all.shape[0]
    return pl.pallas_call(
        _ada_kernel,
        grid=(DEPTH, 3 * D_MODEL // ADA_TN),
        in_specs=[pl.BlockSpec((rows, D_MODEL), lambda l, j: (0, 0)),
                  pl.BlockSpec((1, D_MODEL, ADA_TN), lambda l, j: (l, 0, j)),
                  pl.BlockSpec((1, 1, ADA_TN), lambda l, j: (l, 0, j))],
        out_specs=pl.BlockSpec((1, rows, ADA_TN), lambda l, j: (l, 0, j)),
        out_shape=jax.ShapeDtypeStruct((DEPTH, rows, 3 * D_MODEL), F32),
        compiler_params=_cparams(2),
        name="ada",
    )(c_all, w_ada, _lvec(b_ada))


MOD_SHIFT, MOD_SCALE, MOD_GATE = 0, 1, 2


def _mod_spec(l, kind, tm):
    tiles_p = N_P // tm
    return pl.BlockSpec((1, 1, 1, MOD_ROWS, D_MODEL), lambda i: (l, kind, i // tiles_p, 0, 0))


def _rms(x, gain):
    ms = jnp.mean(x * x, axis=-1, keepdims=True)
    return x * lax.rsqrt(ms + EPS) * gain


IO_TM = TM
IO_TT = IO_TM // B_P
IO_TILES_P = N_P // IO_TM

_prompt_io_spec = pl.BlockSpec((B_P, IO_TT, D_MODEL), lambda i: (0, jnp.minimum(i, IO_TILES_P - 1), 0))
_sample_io_spec = pl.BlockSpec((IO_TM, D_MODEL), lambda i: (jnp.maximum(i - IO_TILES_P, 0), 0))


def _embed_kernel(xp_ref, xs_ref, g_ref, sc_ref, sh_ref, xo_ref, xn_ref):
    i = pl.program_id(0)

    @pl.when(i < IO_TILES_P)
    def _prompt():
        xo_ref[...] = jnp.swapaxes(xp_ref[...], 0, 1).reshape(IO_TM, D_MODEL)

    @pl.when(i >= IO_TILES_P)
    def _sample():
        xo_ref[...] = xs_ref[...]

    for s in range(IO_TM // MOD_ROWS):
        rows = slice(s * MOD_ROWS, (s + 1) * MOD_ROWS)
        y = _rms(xo_ref[rows, :], g_ref[0])
        xn_ref[rows, :] = (y * (1.0 + sc_ref[0, 0, 0]) + sh_ref[0, 0, 0]).astype(BF16)


def _embed(x_prompt, x_sample, gain, mods):
    row = pl.BlockSpec((IO_TM, D_MODEL), lambda i: (i, 0))
    return pl.pallas_call(
        _embed_kernel,
        grid=(N_ROWS // IO_TM,),
        in_specs=[_prompt_io_spec, _sample_io_spec, _lspec(0, (1, D_MODEL)),
                  _mod_spec(0, MOD_SCALE, IO_TM), _mod_spec(0, MOD_SHIFT, IO_TM)],
        out_specs=[row, row],
        out_shape=[jax.ShapeDtypeStruct((N_ROWS, D_MODEL), F32),
                   jax.ShapeDtypeStruct((N_ROWS, D_MODEL), BF16)],
        compiler_params=_cparams(1),
        name="embed",
    )(x_prompt, x_sample, gain, mods, mods)


PROJ_TN = 1024
PROJ_TM = N_ROWS // 8
assert N_ROWS % PROJ_TM == 0 and PROJ_TM % (2 * SUB) == 0


def _proj_kernel(xn_ref, w_ref, b_ref, o_ref, wbf_ref):
    @pl.when(pl.program_id(1) == 0)
    def _():
        wbf_ref[...] = w_ref[0].astype(BF16)

    o_ref[...] = _dot(xn_ref[...], wbf_ref[...]) + b_ref[0]


def _proj(xn, w_in, b_in, l):
    return pl.pallas_call(
        _proj_kernel,
        grid=(IN_COLS // PROJ_TN, N_ROWS // PROJ_TM),
        in_specs=[pl.BlockSpec((PROJ_TM, D_MODEL), lambda j, i: (i, 0)),
                  pl.BlockSpec((1, D_MODEL, PROJ_TN), lambda j, i: (l, 0, j)),
                  pl.BlockSpec((1, 1, PROJ_TN), lambda j, i: (l, 0, j))],
        out_specs=pl.BlockSpec((PROJ_TM, PROJ_TN), lambda j, i: (i, j)),
        out_shape=jax.ShapeDtypeStruct((N_ROWS, IN_COLS), F32),
        scratch_shapes=[pltpu.VMEM((D_MODEL, PROJ_TN), BF16)],
        compiler_params=_cparams(2),
        name="proj_in",
    )(xn, w_in, b_in)


def _gelu(y):
    return 0.5 * y * (1.0 + lax.erf(y * (2.0 ** -0.5)))


def _s5_kernel(u_ref, z_ref, wbu_ref, wc_ref, ar_ref, ai_ref, d_ref,
               wglu_ref, bglu_ref, h0re_ref, h0im_ref,
               ya_ref, hp_ref, hsre_ref, hsim_ref,
               bu_ref, y_ref, g_ref, p_ref, wglu_bf_ref):
    i = pl.program_id(0)
    half = SUB // 2

    @pl.when(i == 0)
    def _init():
        wglu_bf_ref[...] = wglu_ref[0].astype(BF16)
        p_ref[...] = jnp.zeros_like(p_ref)

    def abar(c):
        lanes = slice(c * S5_CN, (c + 1) * S5_CN)
        a_re = jnp.broadcast_to(ar_ref[0, :, lanes], (SUB, S5_CN))
        a_im = jnp.broadcast_to(ai_ref[0, :, lanes], (SUB, S5_CN))
        return a_re, a_im

    @pl.when(i < NP_TILES)
    def _prompt():
        groups = TM // SUB
        lo3 = lax.broadcasted_iota(jnp.int32, (groups, SUB, S5_CK), 1) < half
        lo = lax.broadcasted_iota(jnp.int32, (SUB, S5_CN), 0) < half
        for c in range(S5_CHUNKS):
            lanes = slice(c * S5_CN, (c + 1) * S5_CN)
            chans = slice(c * S5_CK, (c + 1) * S5_CK)
            buf = bu_ref.at[c % 2]
            v = u_ref[:, chans].reshape(groups, SUB, S5_CK)
            r = pltpu.roll(v, half, 1)
            zero = jnp.zeros_like(v)
            t_even = jnp.concatenate([jnp.where(lo3, v, zero), jnp.where(lo3, zero, r)], axis=-1)
            t_odd = jnp.concatenate([jnp.where(lo3, r, zero), jnp.where(lo3, zero, v)], axis=-1)
            u2 = jnp.stack([t_even, t_odd], axis=1).reshape(2 * TM, 2 * S5_CK)
            buf[...] = _dot(u2.astype(BF16), wbu_ref[0, c])

            a_re, a_im = abar(c)
            a_sw = jnp.where(lo, -a_im, a_im)
            p = p_ref[:, lanes]
            for t in range(TM // B_P):
                rows = slice(t * SUB, (t + 1) * SUB)
                p = a_re * p + a_sw * _roll4(p) + buf[rows, :]
                buf[rows, :] = p
            p_ref[:, lanes] = p

            q = _dot(buf[...].astype(BF16), wc_ref[0, c])
            q = q.reshape(2 * groups, SUB, 2 * S5_CK)
            dd = q[:, :, :S5_CK] - pltpu.roll(q[:, :, S5_CK:], half, 1)
            dd = dd.reshape(groups, 2, SUB, S5_CK)
            yv = jnp.where(lo3, dd[:, 0], pltpu.roll(dd[:, 1], half, 1))
            y_ref[:, chans] = yv.reshape(TM, S5_CK)

    @pl.when(i == NP_TILES)
    def _sample():
        for c in range(S5_CHUNKS):
            lanes = slice(c * S5_CN, (c + 1) * S5_CN)
            chans = slice(c * S5_CK, (c + 1) * S5_CK)
            buf = bu_ref.at[c % 2]
            ub = u_ref[:, chans].astype(BF16)
            buf[0:TM, :] = _dot(ub, wbu_ref[0, c, 0:S5_CK, :])
            buf[TM:2 * TM, :] = _dot(ub, wbu_ref[0, c, S5_CK:2 * S5_CK, :])
            a_re, a_im = abar(c)

            def body(g, carry, a_re=a_re, a_im=a_im, lanes=lanes, buf=buf):
                r0 = pl.multiple_of(g * SUB, SUB)
                xr = h0re_ref[0, pl.ds(r0, SUB), lanes]
                xi = h0im_ref[0, pl.ds(r0, SUB), lanes]
                for t in range(T_S):
                    rre = pl.ds(t * B_S + r0, SUB)
                    rim = pl.ds(TM + t * B_S + r0, SUB)
                    xr, xi = (a_re * xr - a_im * xi + buf[rre, :], a_re * xi + a_im * xr + buf[rim, :])
                    buf[rre, :] = xr
                    buf[rim, :] = xi
                hsre_ref[pl.ds(r0, SUB), lanes] = xr
                hsim_ref[pl.ds(r0, SUB), lanes] = xi
                return carry

            lax.fori_loop(0, B_S // SUB, body, 0)
            y_ref[:, chans] = (_dot(buf[0:TM, :].astype(BF16), wc_ref[0, c, :, 0:S5_CK])
                               - _dot(buf[TM:2 * TM, :].astype(BF16), wc_ref[0, c, :, S5_CK:2 * S5_CK]))

    @pl.when(i == NP_TILES - 1)
    def _prompt_state():
        hp_ref[...] = p_ref[...]

    rc = 64

    def act(j, carry):
        rows = pl.ds(pl.multiple_of(j * rc, rc), rc)
        y_ref[rows, :] = _gelu(y_ref[rows, :] + d_ref[0] * u_ref[rows, :])
        return carry

    lax.fori_loop(0, TM // rc, act, 0)
    g_ref[...] = _dot(y_ref[...].astype(BF16), wglu_bf_ref[...])

    def gate(j, carry):
        rows = pl.ds(pl.multiple_of(j * rc, rc), rc)
        y = y_ref[rows, :]
        y = y * _sigmoid(g_ref[rows, :] + bglu_ref[0])
        z = z_ref[rows, :]
        ya_ref[rows, :] = (y * (z * _sigmoid(z))).astype(BF16)
        return carry

    lax.fori_loop(0, TM // rc, gate, 0)


def _s5_branch(proj, wbu, wc, a_re, a_im, d, w_glu, b_glu, h0_re, h0_im, l):
    const2 = lambda i: (0, 0)
    state_s = pl.BlockSpec((B_S, S5_N), const2)
    once = dict(pipeline_mode=pl.Buffered(1))
    lconst = lambda shape: pl.BlockSpec((1,) + shape, lambda i: (l,) + (0,) * len(shape), **once)
    return pl.pallas_call(
        _s5_kernel,
        grid=(N_TILES,),
        in_specs=[pl.BlockSpec((TM, S5_WIDTH), lambda i: (i, 0)),
                  pl.BlockSpec((TM, S5_WIDTH), lambda i: (i, 1)),
                  lconst((S5_CHUNKS, 2 * S5_CK, S5_CN)),
                  lconst((S5_CHUNKS, S5_CN, 2 * S5_CK)),
                  _lspec(l, (1, S5_N)),
                  _lspec(l, (1, S5_N)),
                  _lspec(l, (1, S5_WIDTH)),
                  lconst((S5_WIDTH, S5_WIDTH)),
                  _lspec(l, (1, S5_WIDTH)),
                  lconst((B_S, S5_N)),
                  lconst((B_S, S5_N))],
        out_specs=[pl.BlockSpec((TM, S5_WIDTH), lambda i: (i, 0)),
                   pl.BlockSpec((SUB, S5_N), const2), state_s, state_s],
        out_shape=[jax.ShapeDtypeStruct((N_ROWS, S5_WIDTH), BF16),
                   jax.ShapeDtypeStruct((SUB, S5_N), F32),
                   jax.ShapeDtypeStruct((B_S, S5_N), F32),
                   jax.ShapeDtypeStruct((B_S, S5_N), F32)],
        scratch_shapes=[pltpu.VMEM((2, 2 * TM, S5_CN), F32),
                        pltpu.VMEM((TM, S5_WIDTH), F32),
                        pltpu.VMEM((TM, S5_WIDTH), F32),
                        pltpu.VMEM((SUB, S5_N), F32),
                        pltpu.VMEM((S5_WIDTH, S5_WIDTH), BF16)],
        compiler_params=_cparams(1),
        name="s5_branch",
    )(proj, proj, wbu, wc, a_re, a_im, d, w_glu, b_glu, h0_re, h0_im)


RG_LC = 512


RGM_TM = 256
RGM_TILES = N_P // RGM_TM
MERGE_TN = 1024
XB_COL, ZB_COL, GA_COL, GB_COL = 1, 2, 3, 4
assert RG_WIDTH == D_MODEL and 2 * S5_WIDTH == RG_WIDTH


def _conv_taps(cw_ref, cb_ref, lanes):
    taps = [jnp.broadcast_to(cw_ref[0, k:k + 1, lanes], (SUB, RG_LC)) for k in range(CONV_W)]
    return taps, jnp.broadcast_to(cb_ref[0, :, lanes], (SUB, RG_LC))


def _rg_gates(cv_ref, a_ref, wg_ref, br_ref, bi_ref, lam_ref, cvb_ref=None):
    for h in range(RG_BLOCKS):
        lanes = slice(h * RG_BLOCK, (h + 1) * RG_BLOCK)
        sp = lam_ref[0, :, lanes]
        c_lam = -RG_C * (jnp.maximum(-sp, 0.0) + jnp.log1p(jnp.exp(-jnp.abs(sp))))
        k = c_lam * (0.5 * LOG2_E)
        cv = cv_ref[:, lanes]
        cvb = cv.astype(BF16) if cvb_ref is None else cvb_ref[:, lanes]
        g = _dot(cvb, wg_ref[0, h])
        a = jnp.exp2(k * jnp.tanh(0.5 * (g[:, :RG_BLOCK] + br_ref[0, :, lanes])) + k)
        gi = _sigmoid(g[:, RG_BLOCK:] + bi_ref[0, :, lanes])
        m2 = 1.0 - a * a
        mult = jnp.where(m2 > 0.0, m2 * lax.rsqrt(m2), 0.0)
        a_ref[:, lanes] = a
        cv_ref[:, lanes] = mult * gi * cv


def _merge_tile(ya_ref, yb_ref, ga_ref, gb_ref, wa_ref, wb_ref, o_ref, chunks=range(D_MODEL // MERGE_TN)):
    for j in chunks:
        cols = slice(j * MERGE_TN, (j + 1) * MERGE_TN)
        pa = _dot(ya_ref[...], wa_ref[0, :, cols])
        pb = _dot(yb_ref[...], wb_ref[0, :, cols])
        gated = _sigmoid2(ga_ref[:, cols]) * pa + _sigmoid2(gb_ref[:, cols]) * pb
        o_ref[:, cols] = (0.5 * gated).astype(BF16)


def _rgm_prompt_kernel(xb_ref, zb_ref, ya_ref, ga_ref, gb_ref, cw_ref, cb_ref, wg_ref, br_ref,
                       bi_ref, lam_ref, wa_ref, wb_ref,
                       m_ref, hp_ref,
                       cv_ref, cn_ref, a_ref, cvb_ref, yb_cur_ref, yb_prev_ref, hx_ref, pv_ref, pr_ref, ps_ref):
    s = pl.program_id(0)
    groups = RGM_TM // SUB

    @pl.when(s == 0)
    def _init():
        for r in (hx_ref, pv_ref, pr_ref, ps_ref, yb_prev_ref, cv_ref, cvb_ref):
            r[...] = jnp.zeros_like(r)

    chunks = [slice(c * RG_LC, (c + 1) * RG_LC) for c in range(RG_WIDTH // RG_LC)]
    lo = lax.broadcasted_iota(jnp.int32, (SUB, RG_LC), 0) < SUB // 2

    _rg_gates(cv_ref, a_ref, wg_ref, br_ref, bi_ref, lam_ref, cvb_ref)
    _merge_tile(ya_ref, yb_prev_ref, ga_ref, gb_ref, wa_ref, wb_ref, m_ref)

    for lanes in chunks:
        (w0, w1, w2, w3), cb = _conv_taps(cw_ref, cb_ref, lanes)
        v_prev, r_prev, s_prev = pv_ref[:, lanes], pr_ref[:, lanes], ps_ref[:, lanes]
        for k in range(groups):
            rows = slice(k * SUB, (k + 1) * SUB)
            v = xb_ref[rows, lanes]
            r = _roll4(v)
            sh = jnp.where(lo, r_prev, r)
            cn_ref[rows, lanes] = cb + w3 * v + w2 * sh + w1 * v_prev + w0 * s_prev
            v_prev, r_prev, s_prev = v, r, sh
        pv_ref[:, lanes], pr_ref[:, lanes], ps_ref[:, lanes] = v_prev, r_prev, s_prev

    xs = [hx_ref[:, lanes] for lanes in chunks]
    for k in range(groups):
        rows = slice(k * SUB, (k + 1) * SUB)
        for c, lanes in enumerate(chunks):
            a = a_ref[rows, lanes]
            b = cv_ref[rows, lanes]
            y = a * _roll4(xs[c]) + b
            z = a * _roll4(y) + b
            xs[c] = jnp.where(lo, y, z)
            cv_ref[rows, lanes] = xs[c]
    for c, lanes in enumerate(chunks):
        hx_ref[:, lanes] = xs[c]

    rc = 64
    for j in range(RGM_TM // rc):
        rows = slice(j * rc, (j + 1) * rc)
        hz = 0.5 * zb_ref[rows, :]
        yb_cur_ref[rows, :] = (cv_ref[rows, :] * (hz * (jnp.tanh(hz) + 1.0))).astype(BF16)
    yb_prev_ref[...] = yb_cur_ref[...]
    cv_ref[...] = cn_ref[...]
    cvb_ref[...] = cn_ref[...].astype(BF16)

    @pl.when(s == RGM_TILES)
    def _prompt_state():
        hp_ref[...] = hx_ref[...]


def _rgm_prompt(proj, ya, conv_w, conv_b, wg, b_r, b_i, lam, w_a, w_b, l):
    vec = _lspec(l, (1, RG_WIDTH))
    last = RGM_TILES - 1
    lag = lambda w, col, d: pl.BlockSpec((RGM_TM, w), lambda s: (jnp.clip(s - d, 0, last), col))
    once = dict(pipeline_mode=pl.Buffered(1))
    carry = pltpu.VMEM((SUB, RG_WIDTH), F32)
    return pl.pallas_call(
        _rgm_prompt_kernel,
        grid=(RGM_TILES + 2,),
        in_specs=[lag(RG_WIDTH, XB_COL, 0), lag(RG_WIDTH, ZB_COL, 1), lag(S5_WIDTH, 0, 2),
                  lag(D_MODEL, GA_COL, 2), lag(D_MODEL, GB_COL, 2),
                  _lspec(l, (CONV_W, RG_WIDTH)), vec,
                  _lspec(l, (RG_BLOCKS, RG_BLOCK, 2 * RG_BLOCK)), vec, vec, vec,
                  pl.BlockSpec((1, S5_WIDTH, D_MODEL), lambda s: (l, 0, 0), **once),
                  pl.BlockSpec((1, RG_WIDTH, D_MODEL), lambda s: (l, 0, 0), **once)],
        out_specs=[lag(D_MODEL, 0, 2), pl.BlockSpec((SUB, RG_WIDTH), lambda s: (0, 0))],
        out_shape=[jax.ShapeDtypeStruct((N_ROWS, D_MODEL), BF16),
                   jax.ShapeDtypeStruct((SUB, RG_WIDTH), F32)],
        scratch_shapes=[pltpu.VMEM((RGM_TM, RG_WIDTH), F32),
                        pltpu.VMEM((RGM_TM, RG_WIDTH), F32),
                        pltpu.VMEM((RGM_TM, RG_WIDTH), F32),
                        pltpu.VMEM((RGM_TM, RG_WIDTH), BF16),
                        pltpu.VMEM((RGM_TM, RG_WIDTH), BF16),
                        pltpu.VMEM((RGM_TM, RG_WIDTH), BF16),
                        carry, carry, carry, carry],
        compiler_params=_cparams(1),
        name="rgm_prompt",
    )(proj, proj, ya, proj, proj, conv_w, conv_b, wg, b_r, b_i, lam, w_a, w_b)


def _rgm_sample_kernel(merged_hbm_ref, xb_ref, zb_ref, ya_ref, ga_ref, gb_ref, cw_ref, cb_ref, wg_ref,
                       br_ref, bi_ref, lam_ref, wa_ref, wb_ref, h0_ref, c0_ref,
                       m_ref, hs_ref,
                       cv_ref, a_ref, yb_ref):
    del merged_hbm_ref

    def src(t, r0, lanes):
        if t >= 0:
            return xb_ref[pl.ds(t * B_S + r0, SUB), lanes]
        return c0_ref[0, pl.ds((CONV_W - 1 + t) * B_S + r0, SUB), lanes]

    for c in range(RG_WIDTH // RG_LC):
        lanes = slice(c * RG_LC, (c + 1) * RG_LC)
        (w0, w1, w2, w3), cb = _conv_taps(cw_ref, cb_ref, lanes)

        def body(g, carry, w0=w0, w1=w1, w2=w2, w3=w3, cb=cb, lanes=lanes):
            r0 = pl.multiple_of(g * SUB, SUB)
            for t in range(T_S):
                cv_ref[pl.ds(t * B_S + r0, SUB), lanes] = (
                    cb + w3 * src(t, r0, lanes) + w2 * src(t - 1, r0, lanes)
                    + w1 * src(t - 2, r0, lanes) + w0 * src(t - 3, r0, lanes))
            return carry

        lax.fori_loop(0, B_S // SUB, body, 0)

    _rg_gates(cv_ref, a_ref, wg_ref, br_ref, bi_ref, lam_ref)

    for c in range(RG_WIDTH // RG_LC):
        lanes = slice(c * RG_LC, (c + 1) * RG_LC)

        def body(g, carry, lanes=lanes):
            r0 = pl.multiple_of(g * SUB, SUB)
            x = h0_ref[0, pl.ds(r0, SUB), lanes]
            for t in range(T_S):
                rows = pl.ds(t * B_S + r0, SUB)
                x = a_ref[rows, lanes] * x + cv_ref[rows, lanes]
                cv_ref[rows, lanes] = x
            hs_ref[pl.ds(r0, SUB), lanes] = x
            return carry

        lax.fori_loop(0, B_S // SUB, body, 0)

    rc = 64

    def out(j, carry):
        rows = pl.ds(pl.multiple_of(j * rc, rc), rc)
        z = zb_ref[rows, :]
        yb_ref[rows, :] = (cv_ref[rows, :] * (z * _sigmoid(z))).astype(BF16)
        return carry

    lax.fori_loop(0, N_S // rc, out, 0)
    _merge_tile(ya_ref, yb_ref, ga_ref, gb_ref, wa_ref, wb_ref, m_ref)


def _rgm_sample(merged, proj, ya, conv_w, conv_b, wg, b_r, b_i, lam, w_a, w_b, h0, conv0, l):
    once = dict(pipeline_mode=pl.Buffered(1))
    tile = N_P // N_S
    rows = lambda w, col: pl.BlockSpec((N_S, w), lambda i: (tile, col), **once)
    lconst = lambda shape: pl.BlockSpec((1,) + shape, lambda i: (l,) + (0,) * len(shape), **once)
    vec = lconst((1, RG_WIDTH))
    return pl.pallas_call(
        _rgm_sample_kernel,
        grid=(1,),
        in_specs=[pl.BlockSpec(memory_space=pl.ANY),
                  rows(RG_WIDTH, XB_COL), rows(RG_WIDTH, ZB_COL), rows(S5_WIDTH, 0),
                  rows(D_MODEL, GA_COL), rows(D_MODEL, GB_COL),
                  lconst((CONV_W, RG_WIDTH)), vec,
                  lconst((RG_BLOCKS, RG_BLOCK, 2 * RG_BLOCK)), vec, vec, vec,
                  lconst((S5_WIDTH, D_MODEL)), lconst((RG_WIDTH, D_MODEL)),
                  lconst((B_S, RG_WIDTH)), lconst(((CONV_W - 1) * B_S, RG_WIDTH))],
        out_specs=[pl.BlockSpec((N_S, D_MODEL), lambda i: (tile, 0)),
                   pl.BlockSpec((B_S, RG_WIDTH), lambda i: (0, 0))],
        out_shape=[jax.ShapeDtypeStruct((N_ROWS, D_MODEL), BF16),
                   jax.ShapeDtypeStruct((B_S, RG_WIDTH), F32)],
        input_output_aliases={0: 0},
        scratch_shapes=[pltpu.VMEM((N_S, RG_WIDTH), F32),
                        pltpu.VMEM((N_S, RG_WIDTH), F32),
                        pltpu.VMEM((N_S, RG_WIDTH), BF16)],
        compiler_params=_cparams(1),
        name="rgm_sample",
    )(merged, proj, proj, ya, proj, proj, conv_w, conv_b, wg, b_r, b_i, lam, w_a, w_b, h0, conv0)


OUT_TM = IO_TM


def _out_kernel(last, m_ref, x_ref, w_ref, gate_ref, g_ref, *rest):
    if last:
        yp_ref, ys_ref, y_ref = rest
    else:
        sc_ref, sh_ref, xo_ref, xn_ref = rest
    i = pl.program_id(0)

    out = _dot(m_ref[...], w_ref[0])
    for s in range(OUT_TM // MOD_ROWS):
        rows = slice(s * MOD_ROWS, (s + 1) * MOD_ROWS)
        x = x_ref[rows, :] + gate_ref[0, 0, 0] * out[rows, :]
        y = _rms(x, g_ref[0])
        if last:
            y_ref[rows, :] = y
        else:
            xo_ref[rows, :] = x
            xn_ref[rows, :] = (y * (1.0 + sc_ref[0, 0, 0]) + sh_ref[0, 0, 0]).astype(BF16)

    if last:
        @pl.when(i < IO_TILES_P)
        def _prompt():
            yp_ref[...] = jnp.swapaxes(y_ref[...].reshape(IO_TT, B_P, D_MODEL), 0, 1)

        @pl.when(i >= IO_TILES_P)
        def _sample():
            ys_ref[...] = y_ref[...]


def _out_layer(merged, x, w_out, mods, l, gain, gain_l):
    last = l + 1 == DEPTH
    row = pl.BlockSpec((OUT_TM, D_MODEL), lambda i: (i, 0))
    in_specs = [row, row,
                pl.BlockSpec((1, D_MODEL, D_MODEL), lambda i: (l, 0, 0),
                             pipeline_mode=pl.Buffered(1)),
                _mod_spec(l, MOD_GATE, OUT_TM),
                _lspec(gain_l, (1, D_MODEL))]
    args = [merged, x, w_out, mods, gain]
    scratch = []
    if last:
        out_specs = [_prompt_io_spec, _sample_io_spec]
        out_shape = [jax.ShapeDtypeStruct((B_P, T_P, D_MODEL), F32),
                     jax.ShapeDtypeStruct((N_S, D_MODEL), F32)]
        scratch.append(pltpu.VMEM((OUT_TM, D_MODEL), F32))
    else:
        in_specs += [_mod_spec(l + 1, MOD_SCALE, OUT_TM), _mod_spec(l + 1, MOD_SHIFT, OUT_TM)]
        args += [mods, mods]
        out_specs = [row, row]
        out_shape = [jax.ShapeDtypeStruct((N_ROWS, D_MODEL), F32),
                     jax.ShapeDtypeStruct((N_ROWS, D_MODEL), BF16)]
    return pl.pallas_call(
        functools.partial(_out_kernel, last),
        grid=(N_ROWS // OUT_TM,),
        in_specs=in_specs,
        out_specs=out_specs,
        out_shape=out_shape,
        scratch_shapes=scratch,
        compiler_params=_cparams(1),
        name="out_last" if last else "out_layer",
    )(*args)


def _batch_major(rows, b, t):
    lead = rows.shape[:-2]
    return jnp.swapaxes(rows.reshape(lead + (t, b, rows.shape[-1])), -3, -2)


def _block_diag_in(bbt):
    gpc = S5_GROUPS // S5_CHUNKS
    w = bbt.reshape(DEPTH, S5_GROUP, S5_CHUNKS, gpc, S5_STATE)
    eye = jnp.eye(gpc, dtype=bbt.dtype)
    w = jnp.einsum("lkcgp,gh->lcgkhp", w, eye)
    return w.reshape(DEPTH, S5_CHUNKS, S5_CK, S5_CN)


def _block_diag_out(cmat):
    gpc = S5_GROUPS // S5_CHUNKS
    w = cmat.reshape(DEPTH, S5_CHUNKS, gpc, S5_GROUP, S5_STATE)
    eye = jnp.eye(gpc, dtype=cmat.dtype)
    w = jnp.einsum("lcgkp,gh->lchpgk", w, eye)
    return w.reshape(DEPTH, S5_CHUNKS, S5_CN, S5_CK)


def _mod_patterns(ada):
    a = ada.reshape(DEPTH, ada.shape[1], 3, D_MODEL).transpose(0, 2, 1, 3)
    p = jnp.tile(a[:, :, :B_P], (1, 1, MOD_ROWS // B_P, 1))
    s = a[:, :, B_P:B_P + B_S]
    return jnp.stack([p, s], axis=2)


def kernel(x_prompt, x_sample, state_s5_re, state_s5_im, state_rglru_h, state_conv,
           c_prompt, c_sample, w_ada, b_ada, norm_gain, w_in, b_in,
           s5_lam_re, s5_lam_im, s5_log_dt, s5_b_re, s5_b_im, s5_c_re, s5_c_im,
           s5_d, s5_w_glu, s5_b_glu, rg_conv_w, rg_conv_b, rg_w_r, rg_b_r, rg_w_i, rg_b_i,
           rg_lam, w_proj_a, w_proj_b, w_out, final_gain):
    sdt = state_s5_re.dtype

    to_lanes = lambda v: v.reshape(DEPTH, 1, S5_N)
    bt = lambda b: b.reshape(DEPTH, S5_N, S5_GROUP).transpose(0, 2, 1)
    log_dt = jnp.repeat(s5_log_dt, S5_STATE, axis=-1)
    a_re, a_im, bbt_re, bbt_im = _s5_prep(to_lanes(s5_lam_re), to_lanes(s5_lam_im),
                                          to_lanes(log_dt), bt(s5_b_re), bt(s5_b_im))
    wbu = jnp.concatenate([_block_diag_in(bbt_re), _block_diag_in(bbt_im)], axis=-2).astype(BF16)
    wc = jnp.concatenate([_block_diag_out(s5_c_re), _block_diag_out(s5_c_im)], axis=-1).astype(BF16)
    wg = jnp.concatenate([rg_w_r, rg_w_i], axis=-1).astype(BF16)
    wa_bf, wb_bf, wo_bf = w_proj_a.astype(BF16), w_proj_b.astype(BF16), w_out.astype(BF16)

    n_c = B_P + B_S
    c_all = jnp.concatenate([c_prompt, c_sample, jnp.zeros((-n_c % SUB, D_MODEL), F32)], axis=0)
    mods = _mod_patterns(_ada(c_all, w_ada, b_ada))

    h0_re = state_s5_re.reshape(DEPTH, B_S, S5_N)
    h0_im = state_s5_im.reshape(DEPTH, B_S, S5_N)
    conv0 = state_conv.transpose(0, 2, 1, 3).reshape(DEPTH, (CONV_W - 1) * B_S, RG_WIDTH)
    gains = jnp.concatenate([norm_gain, final_gain[None]], axis=0).reshape(DEPTH + 1, 1, D_MODEL)

    xs_tm = x_sample.transpose(1, 0, 2).reshape(N_S, D_MODEL)
    x, xn = _embed(x_prompt, xs_tm, gains, mods)

    outs = {k: [] for k in ("s5re_p", "s5im_p", "h_p", "conv_p", "s5re_s", "s5im_s", "h_s", "conv_s")}
    y_prompt = y_sample = None
    half = SUB // 2
    xb_cols = slice(2 * S5_WIDTH, 2 * S5_WIDTH + RG_WIDTH)
    tail = CONV_W - 1
    for l in range(DEPTH):
        proj = _proj(xn, w_in, _lvec(b_in), l)
        ya, hp_s5, hs_re, hs_im = _s5_branch(
            proj, wbu, wc, a_re, a_im, _lvec(s5_d), s5_w_glu, _lvec(s5_b_glu), h0_re, h0_im, l)
        rg_params = (rg_conv_w, _lvec(rg_conv_b), wg, _lvec(rg_b_r), _lvec(rg_b_i), _lvec(rg_lam),
                     wa_bf, wb_bf)
        merged, hp = _rgm_prompt(proj, ya, *rg_params, l)
        merged, hs = _rgm_sample(merged, proj, ya, *rg_params, state_rglru_h, conv0, l)
        res = _out_layer(merged, x, wo_bf, mods, l, gains, l + 1)
        if l + 1 < DEPTH:
            x, xn = res
        else:
            y_prompt, y_sample = res[0], _batch_major(res[1], B_S, T_S)

        outs["s5re_p"].append(hp_s5[:half])
        outs["s5im_p"].append(hp_s5[half:])
        outs["h_p"].append(hp[half:])
        outs["conv_p"].append(proj[N_P - tail * B_P:N_P, xb_cols])
        outs["s5re_s"].append(hs_re)
        outs["s5im_s"].append(hs_im)
        outs["h_s"].append(hs)
        outs["conv_s"].append(proj[N_ROWS - tail * B_S:, xb_cols])

    st = lambda k: jnp.stack(outs[k]).astype(sdt)
    s5_shape = lambda b: (DEPTH, b, S5_GROUPS, S5_STATE)
    return (y_prompt, y_sample,
            st("s5re_p").reshape(s5_shape(B_P)), st("s5im_p").reshape(s5_shape(B_P)),
            st("h_p"), _batch_major(st("conv_p"), B_P, tail),
            st("s5re_s").reshape(s5_shape(B_S)), st("s5im_s").reshape(s5_shape(B_S)),
            st("h_s"), _batch_major(st("conv_s"), B_S, tail))
```

```python
import functools

import jax
import jax.numpy as jnp
from jax import lax
from jax.experimental import pallas as pl
from jax.experimental.pallas import tpu as pltpu

F32 = jnp.float32
BF16 = jnp.bfloat16

D_MODEL = 2048
DEPTH = 4
B_P, T_P = 4, 2048
B_S, T_S = 128, 4
S5_WIDTH = 1024
S5_GROUP = 16
S5_GROUPS = 64
S5_STATE = 64
S5_N = S5_GROUPS * S5_STATE
RG_WIDTH = 2048
RG_BLOCKS = 16
RG_BLOCK = 128
RG_C = 8.0
CONV_W = 4
IN_COLS = 2 * S5_WIDTH + 2 * RG_WIDTH + 2 * D_MODEL
EPS = 1e-6
LOG2_E = 1.4426950408889634

TM = 512
N_P = B_P * T_P
N_S = B_S * T_S
N_ROWS = N_P + N_S
NP_TILES = N_P // TM
N_TILES = N_ROWS // TM
MOD_ROWS = 128
S5_CHUNKS = 8
S5_CK = S5_WIDTH // S5_CHUNKS
S5_CN = S5_N // S5_CHUNKS
SUB = 8
VMEM_LIMIT = 56 * 1024 * 1024

assert N_S == TM and N_P % TM == 0 and TM % MOD_ROWS == 0


def _cparams(n_axes, vmem=VMEM_LIMIT):
    return pltpu.CompilerParams(dimension_semantics=("arbitrary",) * n_axes,
                                vmem_limit_bytes=vmem)


def _dot(a, b):
    return jnp.dot(a, b, preferred_element_type=F32)


def _roll4(x):
    return pltpu.roll(x, SUB // 2, 0)


def _sigmoid(x):
    return 0.5 * jnp.tanh(0.5 * x) + 0.5


def _sigmoid2(x):
    return jnp.tanh(0.5 * x) + 1.0


def _lvec(a):
    return a.reshape(DEPTH, 1, a.shape[-1])


def _lspec(l, shape, n_axes=1):
    zeros = (0,) * len(shape)
    if n_axes == 1:
        return pl.BlockSpec((1,) + tuple(shape), lambda i: (l,) + zeros)
    return pl.BlockSpec((1,) + tuple(shape), lambda j, i: (l,) + zeros)


def _s5_prep_kernel(lre_ref, lim_ref, ldt_ref, bre_ref, bim_ref,
                    ar_ref, ai_ref, bbre_ref, bbim_ref):
    lam_re = lre_ref[0]
    lam_im = lim_ref[0]
    dt = jnp.exp(ldt_ref[0])
    mag = jnp.exp(lam_re * dt)
    a_re = mag * jnp.cos(lam_im * dt)
    a_im = mag * jnp.sin(lam_im * dt)
    nr = a_re - 1.0
    ni = a_im
    den = lam_re * lam_re + lam_im * lam_im
    c_re = (nr * lam_re + ni * lam_im) / den
    c_im = (ni * lam_re - nr * lam_im) / den
    b_re = bre_ref[0]
    b_im = bim_ref[0]
    bbre_ref[0] = c_re * b_re - c_im * b_im
    bbim_ref[0] = c_re * b_im + c_im * b_re
    ar_ref[0] = a_re
    ai_ref[0] = a_im


def _s5_prep(lam_re, lam_im, log_dt, bt_re, bt_im):
    vec = pl.BlockSpec((1, 1, S5_N), lambda l: (l, 0, 0))
    mat = pl.BlockSpec((1, S5_GROUP, S5_N), lambda l: (l, 0, 0))
    return pl.pallas_call(
        _s5_prep_kernel,
        grid=(DEPTH,),
        in_specs=[vec, vec, vec, mat, mat],
        out_specs=[vec, vec, mat, mat],
        out_shape=[jax.ShapeDtypeStruct((DEPTH, 1, S5_N), F32)] * 2
        + [jax.ShapeDtypeStruct((DEPTH, S5_GROUP, S5_N), F32)] * 2,
        compiler_params=_cparams(1),
        name="s5_prep",
    )(lam_re, lam_im, log_dt, bt_re, bt_im)


ADA_TN = 1024


def _ada_kernel(c_ref, w_ref, b_ref, o_ref):
    o_ref[0] = _dot(c_ref[...].astype(BF16), w_ref[0].astype(BF16)) + b_ref[0]


def _ada(c_all, w_ada, b_ada):
    rows = c_all.shape[0]
    return pl.pallas_call(
        _ada_kernel,
        grid=(DEPTH, 3 * D_MODEL // ADA_TN),
        in_specs=[pl.BlockSpec((rows, D_MODEL), lambda l, j: (0, 0)),
                  pl.BlockSpec((1, D_MODEL, ADA_TN), lambda l, j: (l, 0, j)),
                  pl.BlockSpec((1, 1, ADA_TN), lambda l, j: (l, 0, j))],
        out_specs=pl.BlockSpec((1, rows, ADA_TN), lambda l, j: (l, 0, j)),
        out_shape=jax.ShapeDtypeStruct((DEPTH, rows, 3 * D_MODEL), F32),
        compiler_params=_cparams(2),
        name="ada",
    )(c_all, w_ada, _lvec(b_ada))


MOD_SHIFT, MOD_SCALE, MOD_GATE = 0, 1, 2


def _mod_spec(l, kind, tm):
    tiles_p = N_P // tm
    return pl.BlockSpec((1, 1, 1, MOD_ROWS, D_MODEL), lambda i: (l, kind, i // tiles_p, 0, 0))


def _rms(x, gain):
    ms = jnp.mean(x * x, axis=-1, keepdims=True)
    return x * lax.rsqrt(ms + EPS) * gain


IO_TM = TM
IO_TT = IO_TM // B_P
IO_TILES_P = N_P // IO_TM

_prompt_io_spec = pl.BlockSpec((B_P, IO_TT, D_MODEL), lambda i: (0, jnp.minimum(i, IO_TILES_P - 1), 0))
_sample_io_spec = pl.BlockSpec((IO_TM, D_MODEL), lambda i: (jnp.maximum(i - IO_TILES_P, 0), 0))


def _embed_kernel(xp_ref, xs_ref, g_ref, sc_ref, sh_ref, xo_ref, xn_ref):
    i = pl.program_id(0)

    @pl.when(i < IO_TILES_P)
    def _prompt():
        xo_ref[...] = jnp.swapaxes(xp_ref[...], 0, 1).reshape(IO_TM, D_MODEL)

    @pl.when(i >= IO_TILES_P)
    def _sample():
        xo_ref[...] = xs_ref[...]

    for s in range(IO_TM // MOD_ROWS):
        rows = slice(s * MOD_ROWS, (s + 1) * MOD_ROWS)
        y = _rms(xo_ref[rows, :], g_ref[0])
        xn_ref[rows, :] = (y * (1.0 + sc_ref[0, 0, 0]) + sh_ref[0, 0, 0]).astype(BF16)


def _embed(x_prompt, x_sample, gain, mods):
    row = pl.BlockSpec((IO_TM, D_MODEL), lambda i: (i, 0))
    return pl.pallas_call(
        _embed_kernel,
        grid=(N_ROWS // IO_TM,),
        in_specs=[_prompt_io_spec, _sample_io_spec, _lspec(0, (1, D_MODEL)),
                  _mod_spec(0, MOD_SCALE, IO_TM), _mod_spec(0, MOD_SHIFT, IO_TM)],
        out_specs=[row, row],
        out_shape=[jax.ShapeDtypeStruct((N_ROWS, D_MODEL), F32),
                   jax.ShapeDtypeStruct((N_ROWS, D_MODEL), BF16)],
        compiler_params=_cparams(1),
        name="embed",
    )(x_prompt, x_sample, gain, mods, mods)


PROJ_TN = 1024
PROJ_TM = N_ROWS // 8
assert N_ROWS % PROJ_TM == 0 and PROJ_TM % (2 * SUB) == 0


def _silu(z):
    hz = 0.5 * z
    return hz * (jnp.tanh(hz) + 1.0)


def _proj_kernel(xn_ref, w_ref, b_ref, o_ref, wbf_ref):
    @pl.when(pl.program_id(1) == 0)
    def _():
        wbf_ref[...] = w_ref[0].astype(BF16)

    o_ref[...] = _dot(xn_ref[...], wbf_ref[...]) + b_ref[0]


def _proj(xn, w_in, b_in, l):
    return pl.pallas_call(
        _proj_kernel,
        grid=(IN_COLS // PROJ_TN, N_ROWS // PROJ_TM),
        in_specs=[pl.BlockSpec((PROJ_TM, D_MODEL), lambda j, i: (i, 0)),
                  pl.BlockSpec((1, D_MODEL, PROJ_TN), lambda j, i: (l, 0, j)),
                  pl.BlockSpec((1, 1, PROJ_TN), lambda j, i: (l, 0, j))],
        out_specs=pl.BlockSpec((PROJ_TM, PROJ_TN), lambda j, i: (i, j)),
        out_shape=jax.ShapeDtypeStruct((N_ROWS, IN_COLS), F32),
        scratch_shapes=[pltpu.VMEM((D_MODEL, PROJ_TN), BF16)],
        compiler_params=_cparams(2),
        name="proj_in",
    )(xn, w_in, b_in)


def _gelu(y):
    return 0.5 * y * (1.0 + lax.erf(y * (2.0 ** -0.5)))


def _s5_kernel(u_ref, z_ref, wbu_ref, wc_ref, ar_ref, ai_ref, d_ref,
               wglu_ref, bglu_ref, h0re_ref, h0im_ref,
               ya_ref, hp_ref, hsre_ref, hsim_ref,
               bu_ref, y_ref, yact_ref, g_ref, p_ref, wglu_bf_ref):
    i = pl.program_id(0)
    half = SUB // 2

    @pl.when(i == 0)
    def _init():
        wglu_bf_ref[...] = wglu_ref[0].astype(BF16)
        p_ref[...] = jnp.zeros_like(p_ref)

    def abar(c):
        lanes = slice(c * S5_CN, (c + 1) * S5_CN)
        a_re = jnp.broadcast_to(ar_ref[0, :, lanes], (SUB, S5_CN))
        a_im = jnp.broadcast_to(ai_ref[0, :, lanes], (SUB, S5_CN))
        return a_re, a_im

    @pl.when(i < NP_TILES)
    def _prompt():
        groups = TM // SUB
        lo3 = lax.broadcasted_iota(jnp.int32, (groups, SUB, S5_CK), 1) < half
        lo = lax.broadcasted_iota(jnp.int32, (SUB, S5_CN), 0) < half
        for c in range(S5_CHUNKS):
            lanes = slice(c * S5_CN, (c + 1) * S5_CN)
            chans = slice(c * S5_CK, (c + 1) * S5_CK)
            buf = bu_ref.at[c % 2]
            v = u_ref[:, chans].reshape(groups, SUB, S5_CK)
            r = pltpu.roll(v, half, 1)
            zero = jnp.zeros_like(v)
            t_even = jnp.concatenate([jnp.where(lo3, v, zero), jnp.where(lo3, zero, r)], axis=-1)
            t_odd = jnp.concatenate([jnp.where(lo3, r, zero), jnp.where(lo3, zero, v)], axis=-1)
            u2 = jnp.stack([t_even, t_odd], axis=1).reshape(2 * TM, 2 * S5_CK)
            buf[...] = _dot(u2.astype(BF16), wbu_ref[0, c])

            a_re, a_im = abar(c)
            a_sw = jnp.where(lo, -a_im, a_im)
            p = p_ref[:, lanes]
            for t in range(TM // B_P):
                rows = slice(t * SUB, (t + 1) * SUB)
                p = a_re * p + a_sw * _roll4(p) + buf[rows, :]
                buf[rows, :] = p
            p_ref[:, lanes] = p

            q = _dot(buf[...].astype(BF16), wc_ref[0, c])
            q = q.reshape(2 * groups, SUB, 2 * S5_CK)
            dd = q[:, :, :S5_CK] - pltpu.roll(q[:, :, S5_CK:], half, 1)
            dd = dd.reshape(groups, 2, SUB, S5_CK)
            yv = jnp.where(lo3, dd[:, 0], pltpu.roll(dd[:, 1], half, 1))
            y_ref[:, chans] = yv.reshape(TM, S5_CK)

    @pl.when(i == NP_TILES)
    def _sample():
        for c in range(S5_CHUNKS):
            lanes = slice(c * S5_CN, (c + 1) * S5_CN)
            chans = slice(c * S5_CK, (c + 1) * S5_CK)
            buf = bu_ref.at[c % 2]
            ub = u_ref[:, chans].astype(BF16)
            buf[0:TM, :] = _dot(ub, wbu_ref[0, c, 0:S5_CK, :])
            buf[TM:2 * TM, :] = _dot(ub, wbu_ref[0, c, S5_CK:2 * S5_CK, :])
            a_re, a_im = abar(c)

            def body(g, carry, a_re=a_re, a_im=a_im, lanes=lanes, buf=buf):
                r0 = pl.multiple_of(g * SUB, SUB)
                xr = h0re_ref[0, pl.ds(r0, SUB), lanes]
                xi = h0im_ref[0, pl.ds(r0, SUB), lanes]
                for t in range(T_S):
                    rre = pl.ds(t * B_S + r0, SUB)
                    rim = pl.ds(TM + t * B_S + r0, SUB)
                    xr, xi = (a_re * xr - a_im * xi + buf[rre, :], a_re * xi + a_im * xr + buf[rim, :])
                    buf[rre, :] = xr
                    buf[rim, :] = xi
                hsre_ref[pl.ds(r0, SUB), lanes] = xr
                hsim_ref[pl.ds(r0, SUB), lanes] = xi
                return carry

            lax.fori_loop(0, B_S // SUB, body, 0)
            y_ref[:, chans] = (_dot(buf[0:TM, :].astype(BF16), wc_ref[0, c, :, 0:S5_CK])
                               - _dot(buf[TM:2 * TM, :].astype(BF16), wc_ref[0, c, :, S5_CK:2 * S5_CK]))

    @pl.when(i == NP_TILES - 1)
    def _prompt_state():
        hp_ref[...] = p_ref[...]

    rc = 64
    halves = [range(h * (TM // 2), (h + 1) * (TM // 2), rc) for h in range(2)]

    def act(r0):
        rows = slice(r0, r0 + rc)
        a = _gelu(y_ref[rows, :] + d_ref[0] * u_ref[rows, :])
        y_ref[rows, :] = a
        yact_ref[rows, :] = a.astype(BF16)

    def glu(h):
        rows = slice(halves[h][0], halves[h][0] + TM // 2)
        g_ref[rows, :] = _dot(yact_ref[rows, :], wglu_bf_ref[...])

    def gate(r0):
        rows = slice(r0, r0 + rc)
        y = y_ref[rows, :] * _sigmoid(g_ref[rows, :] + bglu_ref[0])
        ya_ref[rows, :] = (y * _silu(z_ref[rows, :])).astype(BF16)

    for r0 in halves[0]:
        act(r0)
    glu(0)
    for r0 in halves[1]:
        act(r0)
    glu(1)
    for r0 in halves[0]:
        gate(r0)
    for r0 in halves[1]:
        gate(r0)


def _s5_branch(proj, wbu, wc, a_re, a_im, d, w_glu, b_glu, h0_re, h0_im, l):
    const2 = lambda i: (0, 0)
    state_s = pl.BlockSpec((B_S, S5_N), const2)
    once = dict(pipeline_mode=pl.Buffered(1))
    lconst = lambda shape: pl.BlockSpec((1,) + shape, lambda i: (l,) + (0,) * len(shape), **once)
    return pl.pallas_call(
        _s5_kernel,
        grid=(N_TILES,),
        in_specs=[pl.BlockSpec((TM, S5_WIDTH), lambda i: (i, 0)),
                  pl.BlockSpec((TM, S5_WIDTH), lambda i: (i, 1)),
                  lconst((S5_CHUNKS, 2 * S5_CK, S5_CN)),
                  lconst((S5_CHUNKS, S5_CN, 2 * S5_CK)),
                  _lspec(l, (1, S5_N)),
                  _lspec(l, (1, S5_N)),
                  _lspec(l, (1, S5_WIDTH)),
                  lconst((S5_WIDTH, S5_WIDTH)),
                  _lspec(l, (1, S5_WIDTH)),
                  lconst((B_S, S5_N)),
                  lconst((B_S, S5_N))],
        out_specs=[pl.BlockSpec((TM, S5_WIDTH), lambda i: (i, 0)),
                   pl.BlockSpec((SUB, S5_N), const2), state_s, state_s],
        out_shape=[jax.ShapeDtypeStruct((N_ROWS, S5_WIDTH), BF16),
                   jax.ShapeDtypeStruct((SUB, S5_N), F32),
                   jax.ShapeDtypeStruct((B_S, S5_N), F32),
                   jax.ShapeDtypeStruct((B_S, S5_N), F32)],
        scratch_shapes=[pltpu.VMEM((2, 2 * TM, S5_CN), F32),
                        pltpu.VMEM((TM, S5_WIDTH), F32),
                        pltpu.VMEM((TM, S5_WIDTH), BF16),
                        pltpu.VMEM((TM, S5_WIDTH), F32),
                        pltpu.VMEM((SUB, S5_N), F32),
                        pltpu.VMEM((S5_WIDTH, S5_WIDTH), BF16)],
        compiler_params=_cparams(1),
        name="s5_branch",
    )(proj, proj, wbu, wc, a_re, a_im, d, w_glu, b_glu, h0_re, h0_im)


RG_LC = 512


RGM_TM = 256
RGM_TILES = N_P // RGM_TM
MERGE_TN = 1024
XB_COL, ZB_COL, GA_COL, GB_COL = 1, 2, 3, 4
assert RG_WIDTH == D_MODEL and 2 * S5_WIDTH == RG_WIDTH


def _conv_taps(cw_ref, cb_ref, lanes):
    taps = [jnp.broadcast_to(cw_ref[0, k:k + 1, lanes], (SUB, RG_LC)) for k in range(CONV_W)]
    return taps, jnp.broadcast_to(cb_ref[0, :, lanes], (SUB, RG_LC))


def _rg_gates(cv_ref, a_ref, wg_ref, br_ref, bi_ref, lam_ref, blocks=range(RG_BLOCKS)):
    for h in blocks:
        lanes = slice(h * RG_BLOCK, (h + 1) * RG_BLOCK)
        sp = lam_ref[0, :, lanes]
        c_lam = -RG_C * (jnp.maximum(-sp, 0.0) + jnp.log1p(jnp.exp(-jnp.abs(sp))))
        k = c_lam * (0.5 * LOG2_E)
        cv = cv_ref[:, lanes]
        g = _dot(cv.astype(BF16), wg_ref[0, h])
        a = jnp.exp2(k * jnp.tanh(0.5 * (g[:, :RG_BLOCK] + br_ref[0, :, lanes])) + k)
        gi = _sigmoid(g[:, RG_BLOCK:] + bi_ref[0, :, lanes])
        m2 = 1.0 - a * a
        mult = jnp.where(m2 > 0.0, m2 * lax.rsqrt(m2), 0.0)
        a_ref[:, lanes] = a
        cv_ref[:, lanes] = mult * gi * cv


def _merge_tile(ya_ref, yb_ref, ga_ref, gb_ref, wa_ref, wb_ref, o_ref, chunks=range(D_MODEL // MERGE_TN)):
    for j in chunks:
        cols = slice(j * MERGE_TN, (j + 1) * MERGE_TN)
        pa = _dot(ya_ref[...], wa_ref[0, :, cols])
        pb = _dot(yb_ref[...], wb_ref[0, :, cols])
        gated = _sigmoid2(ga_ref[:, cols]) * pa + _sigmoid2(gb_ref[:, cols]) * pb
        o_ref[:, cols] = (0.5 * gated).astype(BF16)


def _rgm_prompt_kernel(xb_ref, zb_ref, ya_ref, ga_ref, gb_ref, cw_ref, cb_ref, wg_ref, br_ref,
                       bi_ref, lam_ref, wa_ref, wb_ref,
                       m_ref, hp_ref,
                       cv_ref, a_ref, yb_ref, pa_ref, pb_ref, hx_ref, pv_ref, pr_ref, ps_ref):
    s = pl.program_id(0)
    groups = RGM_TM // SUB
    n_lc = RG_WIDTH // RG_LC

    @pl.when(s == 0)
    def _init():
        for r in (hx_ref, pv_ref, pr_ref, ps_ref):
            r[...] = jnp.zeros_like(r)

    lo = lax.broadcasted_iota(jnp.int32, (SUB, RG_LC), 0) < SUB // 2
    pa_ref[...] = _dot(ya_ref[...], wa_ref[0])
    for c in range(n_lc):
        lanes = slice(c * RG_LC, (c + 1) * RG_LC)
        (w0, w1, w2, w3), cb = _conv_taps(cw_ref, cb_ref, lanes)
        v_prev, r_prev, s_prev = pv_ref[:, lanes], pr_ref[:, lanes], ps_ref[:, lanes]
        for k in range(groups):
            rows = slice(k * SUB, (k + 1) * SUB)
            v = xb_ref[rows, lanes]
            r = _roll4(v)
            sh = jnp.where(lo, r_prev, r)
            cv_ref[rows, lanes] = cb + w3 * v + w2 * sh + w1 * v_prev + w0 * s_prev
            v_prev, r_prev, s_prev = v, r, sh
        pv_ref[:, lanes], pr_ref[:, lanes], ps_ref[:, lanes] = v_prev, r_prev, s_prev

        _rg_gates(cv_ref, a_ref, wg_ref, br_ref, bi_ref, lam_ref,
                  blocks=range(c * RG_BLOCKS // n_lc, (c + 1) * RG_BLOCKS // n_lc))

        x = hx_ref[:, lanes]
        for k in range(groups):
            rows = slice(k * SUB, (k + 1) * SUB)
            a = a_ref[rows, lanes]
            b = cv_ref[rows, lanes]
            y = a * _roll4(x) + b
            z = a * _roll4(y) + b
            x = jnp.where(lo, y, z)
            cv_ref[rows, lanes] = x
        hx_ref[:, lanes] = x

        yb_ref[:, lanes] = (cv_ref[:, lanes] * _silu(zb_ref[:, lanes])).astype(BF16)
        part = _dot(yb_ref[:, lanes], wb_ref[0, lanes, :])
        if c == 0:
            pb_ref[...] = part
        else:
            pb_ref[...] += part

    rc = 64
    for j in range(RGM_TM // rc):
        rows = slice(j * rc, (j + 1) * rc)
        gated = _sigmoid2(ga_ref[rows, :]) * pa_ref[rows, :] + _sigmoid2(gb_ref[rows, :]) * pb_ref[rows, :]
        m_ref[rows, :] = (0.5 * gated).astype(BF16)

    @pl.when(s == RGM_TILES - 1)
    def _prompt_state():
        hp_ref[...] = hx_ref[...]


def _rgm_prompt(proj, ya, conv_w, conv_b, wg, b_r, b_i, lam, w_a, w_b, l):
    vec = _lspec(l, (1, RG_WIDTH))
    tile = lambda w, col: pl.BlockSpec((RGM_TM, w), lambda s: (s, col))
    once = dict(pipeline_mode=pl.Buffered(1))
    carry = pltpu.VMEM((SUB, RG_WIDTH), F32)
    return pl.pallas_call(
        _rgm_prompt_kernel,
        grid=(RGM_TILES,),
        in_specs=[tile(RG_WIDTH, XB_COL), tile(RG_WIDTH, ZB_COL), tile(S5_WIDTH, 0),
                  tile(D_MODEL, GA_COL), tile(D_MODEL, GB_COL),
                  _lspec(l, (CONV_W, RG_WIDTH)), vec,
                  _lspec(l, (RG_BLOCKS, RG_BLOCK, 2 * RG_BLOCK)), vec, vec, vec,
                  pl.BlockSpec((1, S5_WIDTH, D_MODEL), lambda s: (l, 0, 0), **once),
                  pl.BlockSpec((1, RG_WIDTH, D_MODEL), lambda s: (l, 0, 0), **once)],
        out_specs=[tile(D_MODEL, 0), pl.BlockSpec((SUB, RG_WIDTH), lambda s: (0, 0))],
        out_shape=[jax.ShapeDtypeStruct((N_ROWS, D_MODEL), BF16),
                   jax.ShapeDtypeStruct((SUB, RG_WIDTH), F32)],
        scratch_shapes=[pltpu.VMEM((RGM_TM, RG_WIDTH), F32),
                        pltpu.VMEM((RGM_TM, RG_WIDTH), F32),
                        pltpu.VMEM((RGM_TM, RG_WIDTH), BF16),
                        pltpu.VMEM((RGM_TM, D_MODEL), F32),
                        pltpu.VMEM((RGM_TM, D_MODEL), F32),
                        carry, carry, carry, carry],
        compiler_params=_cparams(1),
        name="rgm_prompt",
    )(proj, proj, ya, proj, proj, conv_w, conv_b, wg, b_r, b_i, lam, w_a, w_b)


def _rgm_sample_kernel(merged_hbm_ref, xb_ref, zb_ref, ya_ref, ga_ref, gb_ref, cw_ref, cb_ref, wg_ref,
                       br_ref, bi_ref, lam_ref, wa_ref, wb_ref, h0_ref, c0_ref,
                       m_ref, hs_ref,
                       cv_ref, a_ref, yb_ref):
    del merged_hbm_ref

    def src(t, r0, lanes):
        if t >= 0:
            return xb_ref[pl.ds(t * B_S + r0, SUB), lanes]
        return c0_ref[0, pl.ds((CONV_W - 1 + t) * B_S + r0, SUB), lanes]

    for c in range(RG_WIDTH // RG_LC):
        lanes = slice(c * RG_LC, (c + 1) * RG_LC)
        (w0, w1, w2, w3), cb = _conv_taps(cw_ref, cb_ref, lanes)

        def body(g, carry, w0=w0, w1=w1, w2=w2, w3=w3, cb=cb, lanes=lanes):
            r0 = pl.multiple_of(g * SUB, SUB)
            for t in range(T_S):
                cv_ref[pl.ds(t * B_S + r0, SUB), lanes] = (
                    cb + w3 * src(t, r0, lanes) + w2 * src(t - 1, r0, lanes)
                    + w1 * src(t - 2, r0, lanes) + w0 * src(t - 3, r0, lanes))
            return carry

        lax.fori_loop(0, B_S // SUB, body, 0)

    _rg_gates(cv_ref, a_ref, wg_ref, br_ref, bi_ref, lam_ref)

    for c in range(RG_WIDTH // RG_LC):
        lanes = slice(c * RG_LC, (c + 1) * RG_LC)

        def body(g, carry, lanes=lanes):
            r0 = pl.multiple_of(g * SUB, SUB)
            x = h0_ref[0, pl.ds(r0, SUB), lanes]
            for t in range(T_S):
                rows = pl.ds(t * B_S + r0, SUB)
                x = a_ref[rows, lanes] * x + cv_ref[rows, lanes]
                cv_ref[rows, lanes] = x
            hs_ref[pl.ds(r0, SUB), lanes] = x
            return carry

        lax.fori_loop(0, B_S // SUB, body, 0)

    rc = 64

    def out(j, carry):
        rows = pl.ds(pl.multiple_of(j * rc, rc), rc)
        yb_ref[rows, :] = (cv_ref[rows, :] * _silu(zb_ref[rows, :])).astype(BF16)
        return carry

    lax.fori_loop(0, N_S // rc, out, 0)
    _merge_tile(ya_ref, yb_ref, ga_ref, gb_ref, wa_ref, wb_ref, m_ref)


def _rgm_sample(merged, proj, ya, conv_w, conv_b, wg, b_r, b_i, lam, w_a, w_b, h0, conv0, l):
    once = dict(pipeline_mode=pl.Buffered(1))
    tile = N_P // N_S
    rows = lambda w, col: pl.BlockSpec((N_S, w), lambda i: (tile, col), **once)
    lconst = lambda shape: pl.BlockSpec((1,) + shape, lambda i: (l,) + (0,) * len(shape), **once)
    vec = lconst((1, RG_WIDTH))
    return pl.pallas_call(
        _rgm_sample_kernel,
        grid=(1,),
        in_specs=[pl.BlockSpec(memory_space=pl.ANY),
                  rows(RG_WIDTH, XB_COL), rows(RG_WIDTH, ZB_COL), rows(S5_WIDTH, 0),
                  rows(D_MODEL, GA_COL), rows(D_MODEL, GB_COL),
                  lconst((CONV_W, RG_WIDTH)), vec,
                  lconst((RG_BLOCKS, RG_BLOCK, 2 * RG_BLOCK)), vec, vec, vec,
                  lconst((S5_WIDTH, D_MODEL)), lconst((RG_WIDTH, D_MODEL)),
                  lconst((B_S, RG_WIDTH)), lconst(((CONV_W - 1) * B_S, RG_WIDTH))],
        out_specs=[pl.BlockSpec((N_S, D_MODEL), lambda i: (tile, 0)),
                   pl.BlockSpec((B_S, RG_WIDTH), lambda i: (0, 0))],
        out_shape=[jax.ShapeDtypeStruct((N_ROWS, D_MODEL), BF16),
                   jax.ShapeDtypeStruct((B_S, RG_WIDTH), F32)],
        input_output_aliases={0: 0},
        scratch_shapes=[pltpu.VMEM((N_S, RG_WIDTH), F32),
                        pltpu.VMEM((N_S, RG_WIDTH), F32),
                        pltpu.VMEM((N_S, RG_WIDTH), BF16)],
        compiler_params=_cparams(1),
        name="rgm_sample",
    )(merged, proj, proj, ya, proj, proj, conv_w, conv_b, wg, b_r, b_i, lam, w_a, w_b, h0, conv0)


OUT_TM = IO_TM


def _out_kernel(last, m_ref, x_ref, w_ref, gate_ref, g_ref, *rest):
    if last:
        yp_ref, ys_ref, y_ref = rest
    else:
        sc_ref, sh_ref, xo_ref, xn_ref = rest
    i = pl.program_id(0)

    out = _dot(m_ref[...], w_ref[0])
    for s in range(OUT_TM // MOD_ROWS):
        rows = slice(s * MOD_ROWS, (s + 1) * MOD_ROWS)
        x = x_ref[rows, :] + gate_ref[0, 0, 0] * out[rows, :]
        y = _rms(x, g_ref[0])
        if last:
            y_ref[rows, :] = y
        else:
            xo_ref[rows, :] = x
            xn_ref[rows, :] = (y * (1.0 + sc_ref[0, 0, 0]) + sh_ref[0, 0, 0]).astype(BF16)

    if last:
        @pl.when(i < IO_TILES_P)
        def _prompt():
            yp_ref[...] = jnp.swapaxes(y_ref[...].reshape(IO_TT, B_P, D_MODEL), 0, 1)

        @pl.when(i >= IO_TILES_P)
        def _sample():
            ys_ref[...] = y_ref[...]


def _out_layer(merged, x, w_out, mods, l, gain, gain_l):
    last = l + 1 == DEPTH
    row = pl.BlockSpec((OUT_TM, D_MODEL), lambda i: (i, 0))
    in_specs = [row, row,
                pl.BlockSpec((1, D_MODEL, D_MODEL), lambda i: (l, 0, 0),
                             pipeline_mode=pl.Buffered(1)),
                _mod_spec(l, MOD_GATE, OUT_TM),
                _lspec(gain_l, (1, D_MODEL))]
    args = [merged, x, w_out, mods, gain]
    scratch = []
    if last:
        out_specs = [_prompt_io_spec, _sample_io_spec]
        out_shape = [jax.ShapeDtypeStruct((B_P, T_P, D_MODEL), F32),
                     jax.ShapeDtypeStruct((N_S, D_MODEL), F32)]
        scratch.append(pltpu.VMEM((OUT_TM, D_MODEL), F32))
    else:
        in_specs += [_mod_spec(l + 1, MOD_SCALE, OUT_TM), _mod_spec(l + 1, MOD_SHIFT, OUT_TM)]
        args += [mods, mods]
        out_specs = [row, row]
        out_shape = [jax.ShapeDtypeStruct((N_ROWS, D_MODEL), F32),
                     jax.ShapeDtypeStruct((N_ROWS, D_MODEL), BF16)]
    return pl.pallas_call(
        functools.partial(_out_kernel, last),
        grid=(N_ROWS // OUT_TM,),
        in_specs=in_specs,
        out_specs=out_specs,
        out_shape=out_shape,
        scratch_shapes=scratch,
        compiler_params=_cparams(1),
        name="out_last" if last else "out_layer",
    )(*args)


def _batch_major(rows, b, t):
    lead = rows.shape[:-2]
    return jnp.swapaxes(rows.reshape(lead + (t, b, rows.shape[-1])), -3, -2)


def _block_diag_in(bbt):
    gpc = S5_GROUPS // S5_CHUNKS
    w = bbt.reshape(DEPTH, S5_GROUP, S5_CHUNKS, gpc, S5_STATE)
    eye = jnp.eye(gpc, dtype=bbt.dtype)
    w = jnp.einsum("lkcgp,gh->lcgkhp", w, eye)
    return w.reshape(DEPTH, S5_CHUNKS, S5_CK, S5_CN)


def _block_diag_out(cmat):
    gpc = S5_GROUPS // S5_CHUNKS
    w = cmat.reshape(DEPTH, S5_CHUNKS, gpc, S5_GROUP, S5_STATE)
    eye = jnp.eye(gpc, dtype=cmat.dtype)
    w = jnp.einsum("lcgkp,gh->lchpgk", w, eye)
    return w.reshape(DEPTH, S5_CHUNKS, S5_CN, S5_CK)


def _mod_patterns(ada):
    a = ada.reshape(DEPTH, ada.shape[1], 3, D_MODEL).transpose(0, 2, 1, 3)
    p = jnp.tile(a[:, :, :B_P], (1, 1, MOD_ROWS // B_P, 1))
    s = a[:, :, B_P:B_P + B_S]
    return jnp.stack([p, s], axis=2)


def kernel(x_prompt, x_sample, state_s5_re, state_s5_im, state_rglru_h, state_conv,
           c_prompt, c_sample, w_ada, b_ada, norm_gain, w_in, b_in,
           s5_lam_re, s5_lam_im, s5_log_dt, s5_b_re, s5_b_im, s5_c_re, s5_c_im,
           s5_d, s5_w_glu, s5_b_glu, rg_conv_w, rg_conv_b, rg_w_r, rg_b_r, rg_w_i, rg_b_i,
           rg_lam, w_proj_a, w_proj_b, w_out, final_gain):
    sdt = state_s5_re.dtype

    to_lanes = lambda v: v.reshape(DEPTH, 1, S5_N)
    bt = lambda b: b.reshape(DEPTH, S5_N, S5_GROUP).transpose(0, 2, 1)
    log_dt = jnp.repeat(s5_log_dt, S5_STATE, axis=-1)
    a_re, a_im, bbt_re, bbt_im = _s5_prep(to_lanes(s5_lam_re), to_lanes(s5_lam_im),
                                          to_lanes(log_dt), bt(s5_b_re), bt(s5_b_im))
    wbu = jnp.concatenate([_block_diag_in(bbt_re), _block_diag_in(bbt_im)], axis=-2).astype(BF16)
    wc = jnp.concatenate([_block_diag_out(s5_c_re), _block_diag_out(s5_c_im)], axis=-1).astype(BF16)
    wg = jnp.concatenate([rg_w_r, rg_w_i], axis=-1).astype(BF16)
    wa_bf, wb_bf, wo_bf = w_proj_a.astype(BF16), w_proj_b.astype(BF16), w_out.astype(BF16)

    n_c = B_P + B_S
    c_all = jnp.concatenate([c_prompt, c_sample, jnp.zeros((-n_c % SUB, D_MODEL), F32)], axis=0)
    mods = _mod_patterns(_ada(c_all, w_ada, b_ada))

    h0_re = state_s5_re.reshape(DEPTH, B_S, S5_N)
    h0_im = state_s5_im.reshape(DEPTH, B_S, S5_N)
    conv0 = state_conv.transpose(0, 2, 1, 3).reshape(DEPTH, (CONV_W - 1) * B_S, RG_WIDTH)
    gains = jnp.concatenate([norm_gain, final_gain[None]], axis=0).reshape(DEPTH + 1, 1, D_MODEL)

    xs_tm = x_sample.transpose(1, 0, 2).reshape(N_S, D_MODEL)
    x, xn = _embed(x_prompt, xs_tm, gains, mods)

    outs = {k: [] for k in ("s5re_p", "s5im_p", "h_p", "conv_p", "s5re_s", "s5im_s", "h_s", "conv_s")}
    y_prompt = y_sample = None
    half = SUB // 2
    xb_cols = slice(2 * S5_WIDTH, 2 * S5_WIDTH + RG_WIDTH)
    tail = CONV_W - 1
    for l in range(DEPTH):
        proj = _proj(xn, w_in, _lvec(b_in), l)
        ya, hp_s5, hs_re, hs_im = _s5_branch(
            proj, wbu, wc, a_re, a_im, _lvec(s5_d), s5_w_glu, _lvec(s5_b_glu), h0_re, h0_im, l)
        rg_params = (rg_conv_w, _lvec(rg_conv_b), wg, _lvec(rg_b_r), _lvec(rg_b_i), _lvec(rg_lam),
                     wa_bf, wb_bf)
        merged, hp = _rgm_prompt(proj, ya, *rg_params, l)
        merged, hs = _rgm_sample(merged, proj, ya, *rg_params, state_rglru_h, conv0, l)
        res = _out_layer(merged, x, wo_bf, mods, l, gains, l + 1)
        if l + 1 < DEPTH:
            x, xn = res
        else:
            y_prompt, y_sample = res[0], _batch_major(res[1], B_S, T_S)

        outs["s5re_p"].append(hp_s5[:half])
        outs["s5im_p"].append(hp_s5[half:])
        outs["h_p"].append(hp[half:])
        outs["conv_p"].append(proj[N_P - tail * B_P:N_P, xb_cols])
        outs["s5re_s"].append(hs_re)
        outs["s5im_s"].append(hs_im)
        outs["h_s"].append(hs)
        outs["conv_s"].append(proj[N_ROWS - tail * B_S:, xb_cols])

    st = lambda k: jnp.stack(outs[k]).astype(sdt)
    s5_shape = lambda b: (DEPTH, b, S5_GROUPS, S5_STATE)
    return (y_prompt, y_sample,
            st("s5re_p").reshape(s5_shape(B_P)), st("s5im_p").reshape(s5_shape(B_P)),
            st("h_p"), _batch_major(st("conv_p"), B_P, tail),
            st("s5re_s").reshape(s5_shape(B_S)), st("s5im_s").reshape(s5_shape(B_S)),
            st("h_s"), _batch_major(st("conv_s"), B_S, tail))
```

```python
import functools

import jax
import jax.numpy as jnp
from jax import lax
from jax.experimental import pallas as pl
from jax.experimental.pallas import tpu as pltpu

F32 = jnp.float32
BF16 = jnp.bfloat16

D_MODEL = 2048
DEPTH = 4
B_P, T_P = 4, 2048
B_S, T_S = 128, 4
S5_WIDTH = 1024
S5_GROUP = 16
S5_GROUPS = 64
S5_STATE = 64
S5_N = S5_GROUPS * S5_STATE
RG_WIDTH = 2048
RG_BLOCKS = 16
RG_BLOCK = 128
RG_C = 8.0
CONV_W = 4
IN_COLS = 2 * S5_WIDTH + 2 * RG_WIDTH + 2 * D_MODEL
EPS = 1e-6
LOG2_E = 1.4426950408889634

TM = 512
N_P = B_P * T_P
N_S = B_S * T_S
N_ROWS = N_P + N_S
NP_TILES = N_P // TM
N_TILES = N_ROWS // TM
MOD_ROWS = 128
S5_CHUNKS = 8
S5_CK = S5_WIDTH // S5_CHUNKS
S5_CN = S5_N // S5_CHUNKS
SUB = 8
VMEM_LIMIT = 56 * 1024 * 1024

assert N_S == TM and N_P % TM == 0 and TM % MOD_ROWS == 0


def _cparams(n_axes, vmem=VMEM_LIMIT):
    return pltpu.CompilerParams(dimension_semantics=("arbitrary",) * n_axes,
                                vmem_limit_bytes=vmem)


def _dot(a, b):
    return jnp.dot(a, b, preferred_element_type=F32)


def _roll4(x):
    return pltpu.roll(x, SUB // 2, 0)


def _sigmoid(x):
    return 0.5 * jnp.tanh(0.5 * x) + 0.5


def _sigmoid2(x):
    return jnp.tanh(0.5 * x) + 1.0


def _lvec(a):
    return a.reshape(DEPTH, 1, a.shape[-1])


def _lspec(l, shape, n_axes=1):
    zeros = (0,) * len(shape)
    if n_axes == 1:
        return pl.BlockSpec((1,) + tuple(shape), lambda i: (l,) + zeros)
    return pl.BlockSpec((1,) + tuple(shape), lambda j, i: (l,) + zeros)


def _s5_prep_kernel(lre_ref, lim_ref, ldt_ref, bre_ref, bim_ref,
                    ar_ref, ai_ref, bbre_ref, bbim_ref):
    lam_re = lre_ref[0]
    lam_im = lim_ref[0]
    dt = jnp.exp(ldt_ref[0])
    mag = jnp.exp(lam_re * dt)
    a_re = mag * jnp.cos(lam_im * dt)
    a_im = mag * jnp.sin(lam_im * dt)
    nr = a_re - 1.0
    ni = a_im
    den = lam_re * lam_re + lam_im * lam_im
    c_re = (nr * lam_re + ni * lam_im) / den
    c_im = (ni * lam_re - nr * lam_im) / den
    b_re = bre_ref[0]
    b_im = bim_ref[0]
    bbre_ref[0] = c_re * b_re - c_im * b_im
    bbim_ref[0] = c_re * b_im + c_im * b_re
    ar_ref[0] = a_re
    ai_ref[0] = a_im


def _s5_prep(lam_re, lam_im, log_dt, bt_re, bt_im):
    vec = pl.BlockSpec((1, 1, S5_N), lambda l: (l, 0, 0))
    mat = pl.BlockSpec((1, S5_GROUP, S5_N), lambda l: (l, 0, 0))
    return pl.pallas_call(
        _s5_prep_kernel,
        grid=(DEPTH,),
        in_specs=[vec, vec, vec, mat, mat],
        out_specs=[vec, vec, mat, mat],
        out_shape=[jax.ShapeDtypeStruct((DEPTH, 1, S5_N), F32)] * 2
        + [jax.ShapeDtypeStruct((DEPTH, S5_GROUP, S5_N), F32)] * 2,
        compiler_params=_cparams(1),
        name="s5_prep",
    )(lam_re, lam_im, log_dt, bt_re, bt_im)


ADA_TN = 1024


def _ada_kernel(c_ref, w_ref, b_ref, o_ref):
    o_ref[0] = _dot(c_ref[...].astype(BF16), w_ref[0].astype(BF16)) + b_ref[0]


def _ada(c_all, w_ada, b_ada):
    rows = c_all.shape[0]
    return pl.pallas_call(
        _ada_kernel,
        grid=(DEPTH, 3 * D_MODEL // ADA_TN),
        in_specs=[pl.BlockSpec((rows, D_MODEL), lambda l, j: (0, 0)),
                  pl.BlockSpec((1, D_MODEL, ADA_TN), lambda l, j: (l, 0, j)),
                  pl.BlockSpec((1, 1, ADA_TN), lambda l, j: (l, 0, j))],
        out_specs=pl.BlockSpec((1, rows, ADA_TN), lambda l, j: (l, 0, j)),
        out_shape=jax.ShapeDtypeStruct((DEPTH, rows, 3 * D_MODEL), F32),
        compiler_params=_cparams(2),
        name="ada",
    )(c_all, w_ada, _lvec(b_ada))


MOD_SHIFT, MOD_SCALE, MOD_GATE = 0, 1, 2


def _mod_spec(l, kind, tm):
    tiles_p = N_P // tm
    return pl.BlockSpec((1, 1, 1, MOD_ROWS, D_MODEL), lambda i: (l, kind, i // tiles_p, 0, 0))


def _rms(x, gain):
    ms = jnp.mean(x * x, axis=-1, keepdims=True)
    return x * lax.rsqrt(ms + EPS) * gain


IO_TM = TM
IO_TT = IO_TM // B_P
IO_TILES_P = N_P // IO_TM

_prompt_io_spec = pl.BlockSpec((B_P, IO_TT, D_MODEL), lambda i: (0, jnp.minimum(i, IO_TILES_P - 1), 0))
_sample_io_spec = pl.BlockSpec((IO_TM, D_MODEL), lambda i: (jnp.maximum(i - IO_TILES_P, 0), 0))


def _embed_kernel(xp_ref, xs_ref, g_ref, sc_ref, sh_ref, xo_ref, xn_ref):
    i = pl.program_id(0)

    @pl.when(i < IO_TILES_P)
    def _prompt():
        xo_ref[...] = jnp.swapaxes(xp_ref[...], 0, 1).reshape(IO_TM, D_MODEL)

    @pl.when(i >= IO_TILES_P)
    def _sample():
        xo_ref[...] = xs_ref[...]

    for s in range(IO_TM // MOD_ROWS):
        rows = slice(s * MOD_ROWS, (s + 1) * MOD_ROWS)
        y = _rms(xo_ref[rows, :], g_ref[0])
        xn_ref[rows, :] = (y * (1.0 + sc_ref[0, 0, 0]) + sh_ref[0, 0, 0]).astype(BF16)


def _embed(x_prompt, x_sample, gain, mods):
    row = pl.BlockSpec((IO_TM, D_MODEL), lambda i: (i, 0))
    return pl.pallas_call(
        _embed_kernel,
        grid=(N_ROWS // IO_TM,),
        in_specs=[_prompt_io_spec, _sample_io_spec, _lspec(0, (1, D_MODEL)),
                  _mod_spec(0, MOD_SCALE, IO_TM), _mod_spec(0, MOD_SHIFT, IO_TM)],
        out_specs=[row, row],
        out_shape=[jax.ShapeDtypeStruct((N_ROWS, D_MODEL), F32),
                   jax.ShapeDtypeStruct((N_ROWS, D_MODEL), BF16)],
        compiler_params=_cparams(1),
        name="embed",
    )(x_prompt, x_sample, gain, mods, mods)


PROJ_TN = 2048
PROJ_TM = N_ROWS // 16
assert N_ROWS % PROJ_TM == 0 and PROJ_TM % (2 * SUB) == 0


def _silu(z):
    hz = 0.5 * z
    return hz * (jnp.tanh(hz) + 1.0)


def _proj_kernel(xn_ref, w_ref, b_ref, o_ref, wbf_ref):
    @pl.when(pl.program_id(1) == 0)
    def _():
        wbf_ref[...] = w_ref[0].astype(BF16)

    o_ref[...] = _dot(xn_ref[...], wbf_ref[...]) + b_ref[0]


def _proj(xn, w_in, b_in, l):
    return pl.pallas_call(
        _proj_kernel,
        grid=(IN_COLS // PROJ_TN, N_ROWS // PROJ_TM),
        in_specs=[pl.BlockSpec((PROJ_TM, D_MODEL), lambda j, i: (i, 0)),
                  pl.BlockSpec((1, D_MODEL, PROJ_TN), lambda j, i: (l, 0, j)),
                  pl.BlockSpec((1, 1, PROJ_TN), lambda j, i: (l, 0, j))],
        out_specs=pl.BlockSpec((PROJ_TM, PROJ_TN), lambda j, i: (i, j)),
        out_shape=jax.ShapeDtypeStruct((N_ROWS, IN_COLS), F32),
        scratch_shapes=[pltpu.VMEM((D_MODEL, PROJ_TN), BF16)],
        compiler_params=_cparams(2),
        name="proj_in",
    )(xn, w_in, b_in)


def _gelu(y):
    return 0.5 * y * (1.0 + lax.erf(y * (2.0 ** -0.5)))


def _s5_kernel(u_ref, z_ref, wbu_ref, wc_ref, ar_ref, ai_ref, d_ref,
               wglu_ref, bglu_ref, h0re_ref, h0im_ref,
               ya_ref, hp_ref, hsre_ref, hsim_ref,
               bu_ref, y_ref, yact_ref, g_ref, p_ref, wglu_bf_ref):
    i = pl.program_id(0)
    half = SUB // 2

    @pl.when(i == 0)
    def _init():
        wglu_bf_ref[...] = wglu_ref[0].astype(BF16)
        p_ref[...] = jnp.zeros_like(p_ref)

    def abar(c):
        lanes = slice(c * S5_CN, (c + 1) * S5_CN)
        a_re = jnp.broadcast_to(ar_ref[0, :, lanes], (SUB, S5_CN))
        a_im = jnp.broadcast_to(ai_ref[0, :, lanes], (SUB, S5_CN))
        return a_re, a_im

    @pl.when(i < NP_TILES)
    def _prompt():
        groups = TM // SUB
        lo3 = lax.broadcasted_iota(jnp.int32, (groups, SUB, S5_CK), 1) < half
        lo = lax.broadcasted_iota(jnp.int32, (SUB, S5_CN), 0) < half
        for c in range(S5_CHUNKS):
            lanes = slice(c * S5_CN, (c + 1) * S5_CN)
            chans = slice(c * S5_CK, (c + 1) * S5_CK)
            buf = bu_ref.at[c % 2]
            v = u_ref[:, chans].reshape(groups, SUB, S5_CK)
            r = pltpu.roll(v, half, 1)
            zero = jnp.zeros_like(v)
            t_even = jnp.concatenate([jnp.where(lo3, v, zero), jnp.where(lo3, zero, r)], axis=-1)
            t_odd = jnp.concatenate([jnp.where(lo3, r, zero), jnp.where(lo3, zero, v)], axis=-1)
            u2 = jnp.stack([t_even, t_odd], axis=1).reshape(2 * TM, 2 * S5_CK)
            buf[...] = _dot(u2.astype(BF16), wbu_ref[0, c])

            a_re, a_im = abar(c)
            a_sw = jnp.where(lo, -a_im, a_im)
            p = p_ref[:, lanes]
            for t in range(TM // B_P):
                rows = slice(t * SUB, (t + 1) * SUB)
                p = a_re * p + a_sw * _roll4(p) + buf[rows, :]
                buf[rows, :] = p
            p_ref[:, lanes] = p

            q = _dot(buf[...].astype(BF16), wc_ref[0, c])
            q = q.reshape(2 * groups, SUB, 2 * S5_CK)
            dd = q[:, :, :S5_CK] - pltpu.roll(q[:, :, S5_CK:], half, 1)
            dd = dd.reshape(groups, 2, SUB, S5_CK)
            yv = jnp.where(lo3, dd[:, 0], pltpu.roll(dd[:, 1], half, 1))
            y_ref[:, chans] = yv.reshape(TM, S5_CK)

    @pl.when(i == NP_TILES)
    def _sample():
        for c in range(S5_CHUNKS):
            lanes = slice(c * S5_CN, (c + 1) * S5_CN)
            chans = slice(c * S5_CK, (c + 1) * S5_CK)
            buf = bu_ref.at[c % 2]
            ub = u_ref[:, chans].astype(BF16)
            buf[0:TM, :] = _dot(ub, wbu_ref[0, c, 0:S5_CK, :])
            buf[TM:2 * TM, :] = _dot(ub, wbu_ref[0, c, S5_CK:2 * S5_CK, :])
            a_re, a_im = abar(c)

            def body(g, carry, a_re=a_re, a_im=a_im, lanes=lanes, buf=buf):
                r0 = pl.multiple_of(g * SUB, SUB)
                xr = h0re_ref[0, pl.ds(r0, SUB), lanes]
                xi = h0im_ref[0, pl.ds(r0, SUB), lanes]
                for t in range(T_S):
                    rre = pl.ds(t * B_S + r0, SUB)
                    rim = pl.ds(TM + t * B_S + r0, SUB)
                    xr, xi = (a_re * xr - a_im * xi + buf[rre, :], a_re * xi + a_im * xr + buf[rim, :])
                    buf[rre, :] = xr
                    buf[rim, :] = xi
                hsre_ref[pl.ds(r0, SUB), lanes] = xr
                hsim_ref[pl.ds(r0, SUB), lanes] = xi
                return carry

            lax.fori_loop(0, B_S // SUB, body, 0)
            y_ref[:, chans] = (_dot(buf[0:TM, :].astype(BF16), wc_ref[0, c, :, 0:S5_CK])
                               - _dot(buf[TM:2 * TM, :].astype(BF16), wc_ref[0, c, :, S5_CK:2 * S5_CK]))

    @pl.when(i == NP_TILES - 1)
    def _prompt_state():
        hp_ref[...] = p_ref[...]

    rc = 64
    halves = [range(h * (TM // 2), (h + 1) * (TM // 2), rc) for h in range(2)]

    def act(r0):
        rows = slice(r0, r0 + rc)
        a = _gelu(y_ref[rows, :] + d_ref[0] * u_ref[rows, :])
        y_ref[rows, :] = a
        yact_ref[rows, :] = a.astype(BF16)

    def glu(h):
        rows = slice(halves[h][0], halves[h][0] + TM // 2)
        g_ref[rows, :] = _dot(yact_ref[rows, :], wglu_bf_ref[...])

    def gate(r0):
        rows = slice(r0, r0 + rc)
        y = y_ref[rows, :] * _sigmoid(g_ref[rows, :] + bglu_ref[0])
        ya_ref[rows, :] = (y * _silu(z_ref[rows, :])).astype(BF16)

    for r0 in halves[0]:
        act(r0)
    glu(0)
    for r0 in halves[1]:
        act(r0)
    glu(1)
    for r0 in halves[0]:
        gate(r0)
    for r0 in halves[1]:
        gate(r0)


def _s5_branch(proj, wbu, wc, a_re, a_im, d, w_glu, b_glu, h0_re, h0_im, l):
    const2 = lambda i: (0, 0)
    state_s = pl.BlockSpec((B_S, S5_N), const2)
    once = dict(pipeline_mode=pl.Buffered(1))
    lconst = lambda shape: pl.BlockSpec((1,) + shape, lambda i: (l,) + (0,) * len(shape), **once)
    return pl.pallas_call(
        _s5_kernel,
        grid=(N_TILES,),
        in_specs=[pl.BlockSpec((TM, S5_WIDTH), lambda i: (i, 0)),
                  pl.BlockSpec((TM, S5_WIDTH), lambda i: (i, 1)),
                  lconst((S5_CHUNKS, 2 * S5_CK, S5_CN)),
                  lconst((S5_CHUNKS, S5_CN, 2 * S5_CK)),
                  _lspec(l, (1, S5_N)),
                  _lspec(l, (1, S5_N)),
                  _lspec(l, (1, S5_WIDTH)),
                  lconst((S5_WIDTH, S5_WIDTH)),
                  _lspec(l, (1, S5_WIDTH)),
                  lconst((B_S, S5_N)),
                  lconst((B_S, S5_N))],
        out_specs=[pl.BlockSpec((TM, S5_WIDTH), lambda i: (i, 0)),
                   pl.BlockSpec((SUB, S5_N), const2), state_s, state_s],
        out_shape=[jax.ShapeDtypeStruct((N_ROWS, S5_WIDTH), BF16),
                   jax.ShapeDtypeStruct((SUB, S5_N), F32),
                   jax.ShapeDtypeStruct((B_S, S5_N), F32),
                   jax.ShapeDtypeStruct((B_S, S5_N), F32)],
        scratch_shapes=[pltpu.VMEM((2, 2 * TM, S5_CN), F32),
                        pltpu.VMEM((TM, S5_WIDTH), F32),
                        pltpu.VMEM((TM, S5_WIDTH), BF16),
                        pltpu.VMEM((TM, S5_WIDTH), F32),
                        pltpu.VMEM((SUB, S5_N), F32),
                        pltpu.VMEM((S5_WIDTH, S5_WIDTH), BF16)],
        compiler_params=_cparams(1),
        name="s5_branch",
    )(proj, proj, wbu, wc, a_re, a_im, d, w_glu, b_glu, h0_re, h0_im)


RG_LC = 512


RGM_TM = 256
RGM_TILES = N_P // RGM_TM
MERGE_TN = 1024
XB_COL, ZB_COL, GA_COL, GB_COL = 1, 2, 3, 4
assert RG_WIDTH == D_MODEL and 2 * S5_WIDTH == RG_WIDTH


def _conv_taps(cw_ref, cb_ref, lanes):
    taps = [jnp.broadcast_to(cw_ref[0, k:k + 1, lanes], (SUB, RG_LC)) for k in range(CONV_W)]
    return taps, jnp.broadcast_to(cb_ref[0, :, lanes], (SUB, RG_LC))


def _rg_gates(cv_ref, a_ref, wg_ref, br_ref, bi_ref, lam_ref, cvb_ref=None):
    for h in range(RG_BLOCKS):
        lanes = slice(h * RG_BLOCK, (h + 1) * RG_BLOCK)
        sp = lam_ref[0, :, lanes]
        c_lam = -RG_C * (jnp.maximum(-sp, 0.0) + jnp.log1p(jnp.exp(-jnp.abs(sp))))
        k = c_lam * (0.5 * LOG2_E)
        cv = cv_ref[:, lanes]
        cvb = cv.astype(BF16) if cvb_ref is None else cvb_ref[:, lanes]
        g = _dot(cvb, wg_ref[0, h])
        a = jnp.exp2(k * jnp.tanh(0.5 * (g[:, :RG_BLOCK] + br_ref[0, :, lanes])) + k)
        gi = _sigmoid(g[:, RG_BLOCK:] + bi_ref[0, :, lanes])
        m2 = 1.0 - a * a
        mult = jnp.where(m2 > 0.0, m2 * lax.rsqrt(m2), 0.0)
        a_ref[:, lanes] = a
        cv_ref[:, lanes] = mult * gi * cv


def _merge_tile(ya_ref, yb_ref, ga_ref, gb_ref, wa_ref, wb_ref, o_ref, chunks=range(D_MODEL // MERGE_TN)):
    for j in chunks:
        cols = slice(j * MERGE_TN, (j + 1) * MERGE_TN)
        pa = _dot(ya_ref[...], wa_ref[0, :, cols])
        pb = _dot(yb_ref[...], wb_ref[0, :, cols])
        gated = _sigmoid2(ga_ref[:, cols]) * pa + _sigmoid2(gb_ref[:, cols]) * pb
        o_ref[:, cols] = (0.5 * gated).astype(BF16)


def _rgm_prompt_kernel(xb_ref, zb_ref, ya_ref, ga_ref, gb_ref, cw_ref, cb_ref, wg_ref, br_ref,
                       bi_ref, lam_ref, wa_ref, wb_ref,
                       m_ref, hp_ref,
                       cv_ref, cn_ref, a_ref, cvb_ref, yb_cur_ref, yb_prev_ref, hx_ref, pv_ref, pr_ref, ps_ref):
    s = pl.program_id(0)
    groups = RGM_TM // SUB

    @pl.when(s == 0)
    def _init():
        for r in (hx_ref, pv_ref, pr_ref, ps_ref, yb_prev_ref, cv_ref, cvb_ref):
            r[...] = jnp.zeros_like(r)

    chunks = [slice(c * RG_LC, (c + 1) * RG_LC) for c in range(RG_WIDTH // RG_LC)]
    lo = lax.broadcasted_iota(jnp.int32, (SUB, RG_LC), 0) < SUB // 2

    _rg_gates(cv_ref, a_ref, wg_ref, br_ref, bi_ref, lam_ref, cvb_ref)
    _merge_tile(ya_ref, yb_prev_ref, ga_ref, gb_ref, wa_ref, wb_ref, m_ref)

    for lanes in chunks:
        (w0, w1, w2, w3), cb = _conv_taps(cw_ref, cb_ref, lanes)
        v_prev, r_prev, s_prev = pv_ref[:, lanes], pr_ref[:, lanes], ps_ref[:, lanes]
        for k in range(groups):
            rows = slice(k * SUB, (k + 1) * SUB)
            v = xb_ref[rows, lanes]
            r = _roll4(v)
            sh = jnp.where(lo, r_prev, r)
            cn_ref[rows, lanes] = cb + w3 * v + w2 * sh + w1 * v_prev + w0 * s_prev
            v_prev, r_prev, s_prev = v, r, sh
        pv_ref[:, lanes], pr_ref[:, lanes], ps_ref[:, lanes] = v_prev, r_prev, s_prev

    xs = [hx_ref[:, lanes] for lanes in chunks]
    for k in range(groups):
        rows = slice(k * SUB, (k + 1) * SUB)
        for c, lanes in enumerate(chunks):
            a = a_ref[rows, lanes]
            b = cv_ref[rows, lanes]
            y = a * _roll4(xs[c]) + b
            z = a * _roll4(y) + b
            xs[c] = jnp.where(lo, y, z)
            cv_ref[rows, lanes] = xs[c]
    for c, lanes in enumerate(chunks):
        hx_ref[:, lanes] = xs[c]

    rc = 64
    for j in range(RGM_TM // rc):
        rows = slice(j * rc, (j + 1) * rc)
        yb_cur_ref[rows, :] = (cv_ref[rows, :] * _silu(zb_ref[rows, :])).astype(BF16)
    yb_prev_ref[...] = yb_cur_ref[...]
    cv_ref[...] = cn_ref[...]
    cvb_ref[...] = cn_ref[...].astype(BF16)

    @pl.when(s == RGM_TILES)
    def _prompt_state():
        hp_ref[...] = hx_ref[...]


def _rgm_prompt(proj, ya, conv_w, conv_b, wg, b_r, b_i, lam, w_a, w_b, l):
    vec = _lspec(l, (1, RG_WIDTH))
    last = RGM_TILES - 1
    lag = lambda w, col, d: pl.BlockSpec((RGM_TM, w), lambda s: (jnp.clip(s - d, 0, last), col))
    once = dict(pipeline_mode=pl.Buffered(1))
    carry = pltpu.VMEM((SUB, RG_WIDTH), F32)
    return pl.pallas_call(
        _rgm_prompt_kernel,
        grid=(RGM_TILES + 2,),
        in_specs=[lag(RG_WIDTH, XB_COL, 0), lag(RG_WIDTH, ZB_COL, 1), lag(S5_WIDTH, 0, 2),
                  lag(D_MODEL, GA_COL, 2), lag(D_MODEL, GB_COL, 2),
                  _lspec(l, (CONV_W, RG_WIDTH)), vec,
                  _lspec(l, (RG_BLOCKS, RG_BLOCK, 2 * RG_BLOCK)), vec, vec, vec,
                  pl.BlockSpec((1, S5_WIDTH, D_MODEL), lambda s: (l, 0, 0), **once),
                  pl.BlockSpec((1, RG_WIDTH, D_MODEL), lambda s: (l, 0, 0), **once)],
        out_specs=[lag(D_MODEL, 0, 2), pl.BlockSpec((SUB, RG_WIDTH), lambda s: (0, 0))],
        out_shape=[jax.ShapeDtypeStruct((N_ROWS, D_MODEL), BF16),
                   jax.ShapeDtypeStruct((SUB, RG_WIDTH), F32)],
        scratch_shapes=[pltpu.VMEM((RGM_TM, RG_WIDTH), F32),
                        pltpu.VMEM((RGM_TM, RG_WIDTH), F32),
                        pltpu.VMEM((RGM_TM, RG_WIDTH), F32),
                        pltpu.VMEM((RGM_TM, RG_WIDTH), BF16),
                        pltpu.VMEM((RGM_TM, RG_WIDTH), BF16),
                        pltpu.VMEM((RGM_TM, RG_WIDTH), BF16),
                        carry, carry, carry, carry],
        compiler_params=_cparams(1),
        name="rgm_prompt",
    )(proj, proj, ya, proj, proj, conv_w, conv_b, wg, b_r, b_i, lam, w_a, w_b)


def _rgm_sample_kernel(merged_hbm_ref, xb_ref, zb_ref, ya_ref, ga_ref, gb_ref, cw_ref, cb_ref, wg_ref,
                       br_ref, bi_ref, lam_ref, wa_ref, wb_ref, h0_ref, c0_ref,
                       m_ref, hs_ref,
                       cv_ref, a_ref, yb_ref):
    del merged_hbm_ref

    def src(t, r0, lanes):
        if t >= 0:
            return xb_ref[pl.ds(t * B_S + r0, SUB), lanes]
        return c0_ref[0, pl.ds((CONV_W - 1 + t) * B_S + r0, SUB), lanes]

    for c in range(RG_WIDTH // RG_LC):
        lanes = slice(c * RG_LC, (c + 1) * RG_LC)
        (w0, w1, w2, w3), cb = _conv_taps(cw_ref, cb_ref, lanes)

        def body(g, carry, w0=w0, w1=w1, w2=w2, w3=w3, cb=cb, lanes=lanes):
            r0 = pl.multiple_of(g * SUB, SUB)
            for t in range(T_S):
                cv_ref[pl.ds(t * B_S + r0, SUB), lanes] = (
                    cb + w3 * src(t, r0, lanes) + w2 * src(t - 1, r0, lanes)
                    + w1 * src(t - 2, r0, lanes) + w0 * src(t - 3, r0, lanes))
            return carry

        lax.fori_loop(0, B_S // SUB, body, 0)

    _rg_gates(cv_ref, a_ref, wg_ref, br_ref, bi_ref, lam_ref)

    for c in range(RG_WIDTH // RG_LC):
        lanes = slice(c * RG_LC, (c + 1) * RG_LC)

        def body(g, carry, lanes=lanes):
            r0 = pl.multiple_of(g * SUB, SUB)
            x = h0_ref[0, pl.ds(r0, SUB), lanes]
            for t in range(T_S):
                rows = pl.ds(t * B_S + r0, SUB)
                x = a_ref[rows, lanes] * x + cv_ref[rows, lanes]
                cv_ref[rows, lanes] = x
            hs_ref[pl.ds(r0, SUB), lanes] = x
            return carry

        lax.fori_loop(0, B_S // SUB, body, 0)

    rc = 64

    def out(j, carry):
        rows = pl.ds(pl.multiple_of(j * rc, rc), rc)
        yb_ref[rows, :] = (cv_ref[rows, :] * _silu(zb_ref[rows, :])).astype(BF16)
        return carry

    lax.fori_loop(0, N_S // rc, out, 0)
    _merge_tile(ya_ref, yb_ref, ga_ref, gb_ref, wa_ref, wb_ref, m_ref)


def _rgm_sample(merged, proj, ya, conv_w, conv_b, wg, b_r, b_i, lam, w_a, w_b, h0, conv0, l):
    once = dict(pipeline_mode=pl.Buffered(1))
    tile = N_P // N_S
    rows = lambda w, col: pl.BlockSpec((N_S, w), lambda i: (tile, col), **once)
    lconst = lambda shape: pl.BlockSpec((1,) + shape, lambda i: (l,) + (0,) * len(shape), **once)
    vec = lconst((1, RG_WIDTH))
    return pl.pallas_call(
        _rgm_sample_kernel,
        grid=(1,),
        in_specs=[pl.BlockSpec(memory_space=pl.ANY),
                  rows(RG_WIDTH, XB_COL), rows(RG_WIDTH, ZB_COL), rows(S5_WIDTH, 0),
                  rows(D_MODEL, GA_COL), rows(D_MODEL, GB_COL),
                  lconst((CONV_W, RG_WIDTH)), vec,
                  lconst((RG_BLOCKS, RG_BLOCK, 2 * RG_BLOCK)), vec, vec, vec,
                  lconst((S5_WIDTH, D_MODEL)), lconst((RG_WIDTH, D_MODEL)),
                  lconst((B_S, RG_WIDTH)), lconst(((CONV_W - 1) * B_S, RG_WIDTH))],
        out_specs=[pl.BlockSpec((N_S, D_MODEL), lambda i: (tile, 0)),
                   pl.BlockSpec((B_S, RG_WIDTH), lambda i: (0, 0))],
        out_shape=[jax.ShapeDtypeStruct((N_ROWS, D_MODEL), BF16),
                   jax.ShapeDtypeStruct((B_S, RG_WIDTH), F32)],
        input_output_aliases={0: 0},
        scratch_shapes=[pltpu.VMEM((N_S, RG_WIDTH), F32),
                        pltpu.VMEM((N_S, RG_WIDTH), F32),
                        pltpu.VMEM((N_S, RG_WIDTH), BF16)],
        compiler_params=_cparams(1),
        name="rgm_sample",
    )(merged, proj, proj, ya, proj, proj, conv_w, conv_b, wg, b_r, b_i, lam, w_a, w_b, h0, conv0)


OUT_TM = IO_TM


def _out_kernel(last, m_ref, x_ref, w_ref, gate_ref, g_ref, *rest):
    if last:
        yp_ref, ys_ref, y_ref = rest
    else:
        sc_ref, sh_ref, xo_ref, xn_ref = rest
    i = pl.program_id(0)

    out = _dot(m_ref[...], w_ref[0])
    for s in range(OUT_TM // MOD_ROWS):
        rows = slice(s * MOD_ROWS, (s + 1) * MOD_ROWS)
        x = x_ref[rows, :] + gate_ref[0, 0, 0] * out[rows, :]
        y = _rms(x, g_ref[0])
        if last:
            y_ref[rows, :] = y
        else:
            xo_ref[rows, :] = x
            xn_ref[rows, :] = (y * (1.0 + sc_ref[0, 0, 0]) + sh_ref[0, 0, 0]).astype(BF16)

    if last:
        @pl.when(i < IO_TILES_P)
        def _prompt():
            yp_ref[...] = jnp.swapaxes(y_ref[...].reshape(IO_TT, B_P, D_MODEL), 0, 1)

        @pl.when(i >= IO_TILES_P)
        def _sample():
            ys_ref[...] = y_ref[...]


def _out_layer(merged, x, w_out, mods, l, gain, gain_l):
    last = l + 1 == DEPTH
    row = pl.BlockSpec((OUT_TM, D_MODEL), lambda i: (i, 0))
    in_specs = [row, row,
                pl.BlockSpec((1, D_MODEL, D_MODEL), lambda i: (l, 0, 0),
                             pipeline_mode=pl.Buffered(1)),
                _mod_spec(l, MOD_GATE, OUT_TM),
                _lspec(gain_l, (1, D_MODEL))]
    args = [merged, x, w_out, mods, gain]
    scratch = []
    if last:
        out_specs = [_prompt_io_spec, _sample_io_spec]
        out_shape = [jax.ShapeDtypeStruct((B_P, T_P, D_MODEL), F32),
                     jax.ShapeDtypeStruct((N_S, D_MODEL), F32)]
        scratch.append(pltpu.VMEM((OUT_TM, D_MODEL), F32))
    else:
        in_specs += [_mod_spec(l + 1, MOD_SCALE, OUT_TM), _mod_spec(l + 1, MOD_SHIFT, OUT_TM)]
        args += [mods, mods]
        out_specs = [row, row]
        out_shape = [jax.ShapeDtypeStruct((N_ROWS, D_MODEL), F32),
                     jax.ShapeDtypeStruct((N_ROWS, D_MODEL), BF16)]
    return pl.pallas_call(
        functools.partial(_out_kernel, last),
        grid=(N_ROWS // OUT_TM,),
        in_specs=in_specs,
        out_specs=out_specs,
        out_shape=out_shape,
        scratch_shapes=scratch,
        compiler_params=_cparams(1),
        name="out_last" if last else "out_layer",
    )(*args)


def _batch_major(rows, b, t):
    lead = rows.shape[:-2]
    return jnp.swapaxes(rows.reshape(lead + (t, b, rows.shape[-1])), -3, -2)


def _block_diag_in(bbt):
    gpc = S5_GROUPS // S5_CHUNKS
    w = bbt.reshape(DEPTH, S5_GROUP, S5_CHUNKS, gpc, S5_STATE)
    eye = jnp.eye(gpc, dtype=bbt.dtype)
    w = jnp.einsum("lkcgp,gh->lcgkhp", w, eye)
    return w.reshape(DEPTH, S5_CHUNKS, S5_CK, S5_CN)


def _block_diag_out(cmat):
    gpc = S5_GROUPS // S5_CHUNKS
    w = cmat.reshape(DEPTH, S5_CHUNKS, gpc, S5_GROUP, S5_STATE)
    eye = jnp.eye(gpc, dtype=cmat.dtype)
    w = jnp.einsum("lcgkp,gh->lchpgk", w, eye)
    return w.reshape(DEPTH, S5_CHUNKS, S5_CN, S5_CK)


def _mod_patterns(ada):
    a = ada.reshape(DEPTH, ada.shape[1], 3, D_MODEL).transpose(0, 2, 1, 3)
    p = jnp.tile(a[:, :, :B_P], (1, 1, MOD_ROWS // B_P, 1))
    s = a[:, :, B_P:B_P + B_S]
    return jnp.stack([p, s], axis=2)


def kernel(x_prompt, x_sample, state_s5_re, state_s5_im, state_rglru_h, state_conv,
           c_prompt, c_sample, w_ada, b_ada, norm_gain, w_in, b_in,
           s5_lam_re, s5_lam_im, s5_log_dt, s5_b_re, s5_b_im, s5_c_re, s5_c_im,
           s5_d, s5_w_glu, s5_b_glu, rg_conv_w, rg_conv_b, rg_w_r, rg_b_r, rg_w_i, rg_b_i,
           rg_lam, w_proj_a, w_proj_b, w_out, final_gain):
    sdt = state_s5_re.dtype

    to_lanes = lambda v: v.reshape(DEPTH, 1, S5_N)
    bt = lambda b: b.reshape(DEPTH, S5_N, S5_GROUP).transpose(0, 2, 1)
    log_dt = jnp.repeat(s5_log_dt, S5_STATE, axis=-1)
    a_re, a_im, bbt_re, bbt_im = _s5_prep(to_lanes(s5_lam_re), to_lanes(s5_lam_im),
                                          to_lanes(log_dt), bt(s5_b_re), bt(s5_b_im))
    wbu = jnp.concatenate([_block_diag_in(bbt_re), _block_diag_in(bbt_im)], axis=-2).astype(BF16)
    wc = jnp.concatenate([_block_diag_out(s5_c_re), _block_diag_out(s5_c_im)], axis=-1).astype(BF16)
    wg = jnp.concatenate([rg_w_r, rg_w_i], axis=-1).astype(BF16)
    wa_bf, wb_bf, wo_bf = w_proj_a.astype(BF16), w_proj_b.astype(BF16), w_out.astype(BF16)

    n_c = B_P + B_S
    c_all = jnp.concatenate([c_prompt, c_sample, jnp.zeros((-n_c % SUB, D_MODEL), F32)], axis=0)
    mods = _mod_patterns(_ada(c_all, w_ada, b_ada))

    h0_re = state_s5_re.reshape(DEPTH, B_S, S5_N)
    h0_im = state_s5_im.reshape(DEPTH, B_S, S5_N)
    conv0 = state_conv.transpose(0, 2, 1, 3).reshape(DEPTH, (CONV_W - 1) * B_S, RG_WIDTH)
    gains = jnp.concatenate([norm_gain, final_gain[None]], axis=0).reshape(DEPTH + 1, 1, D_MODEL)

    xs_tm = x_sample.transpose(1, 0, 2).reshape(N_S, D_MODEL)
    x, xn = _embed(x_prompt, xs_tm, gains, mods)

    outs = {k: [] for k in ("s5re_p", "s5im_p", "h_p", "conv_p", "s5re_s", "s5im_s", "h_s", "conv_s")}
    y_prompt = y_sample = None
    half = SUB // 2
    xb_cols = slice(2 * S5_WIDTH, 2 * S5_WIDTH + RG_WIDTH)
    tail = CONV_W - 1
    for l in range(DEPTH):
        proj = _proj(xn, w_in, _lvec(b_in), l)
        ya, hp_s5, hs_re, hs_im = _s5_branch(
            proj, wbu, wc, a_re, a_im, _lvec(s5_d), s5_w_glu, _lvec(s5_b_glu), h0_re, h0_im, l)
        rg_params = (rg_conv_w, _lvec(rg_conv_b), wg, _lvec(rg_b_r), _lvec(rg_b_i), _lvec(rg_lam),
                     wa_bf, wb_bf)
        merged, hp = _rgm_prompt(proj, ya, *rg_params, l)
        merged, hs = _rgm_sample(merged, proj, ya, *rg_params, state_rglru_h, conv0, l)
        res = _out_layer(merged, x, wo_bf, mods, l, gains, l + 1)
        if l + 1 < DEPTH:
            x, xn = res
        else:
            y_prompt, y_sample = res[0], _batch_major(res[1], B_S, T_S)

        outs["s5re_p"].append(hp_s5[:half])
        outs["s5im_p"].append(hp_s5[half:])
        outs["h_p"].append(hp[half:])
        outs["conv_p"].append(proj[N_P - tail * B_P:N_P, xb_cols])
        outs["s5re_s"].append(hs_re)
        outs["s5im_s"].append(hs_im)
        outs["h_s"].append(hs)
        outs["conv_s"].append(proj[N_ROWS - tail * B_S:, xb_cols])

    st = lambda k: jnp.stack(outs[k]).astype(sdt)
    s5_shape = lambda b: (DEPTH, b, S5_GROUPS, S5_STATE)
    return (y_prompt, y_sample,
            st("s5re_p").reshape(s5_shape(B_P)), st("s5im_p").reshape(s5_shape(B_P)),
            st("h_p"), _batch_major(st("conv_p"), B_P, tail),
            st("s5re_s").reshape(s5_shape(B_S)), st("s5im_s").reshape(s5_shape(B_S)),
            st("h_s"), _batch_major(st("conv_s"), B_S, tail))
```

```python
import functools

import jax
import jax.numpy as jnp
from jax import lax
from jax.experimental import pallas as pl
from jax.experimental.pallas import tpu as pltpu

F32 = jnp.float32
BF16 = jnp.bfloat16

D_MODEL = 2048
DEPTH = 4
B_P, T_P = 4, 2048
B_S, T_S = 128, 4
S5_WIDTH = 1024
S5_GROUP = 16
S5_GROUPS = 64
S5_STATE = 64
S5_N = S5_GROUPS * S5_STATE
RG_WIDTH = 2048
RG_BLOCKS = 16
RG_BLOCK = 128
RG_C = 8.0
CONV_W = 4
IN_COLS = 2 * S5_WIDTH + 2 * RG_WIDTH + 2 * D_MODEL
EPS = 1e-6
LOG2_E = 1.4426950408889634

TM = 512
N_P = B_P * T_P
N_S = B_S * T_S
N_ROWS = N_P + N_S
NP_TILES = N_P // TM
N_TILES = N_ROWS // TM
MOD_ROWS = 128
S5_CHUNKS = 8
S5_CK = S5_WIDTH // S5_CHUNKS
S5_CN = S5_N // S5_CHUNKS
SUB = 8
VMEM_LIMIT = 56 * 1024 * 1024

assert N_S == TM and N_P % TM == 0 and TM % MOD_ROWS == 0


def _cparams(n_axes, vmem=VMEM_LIMIT):
    return pltpu.CompilerParams(dimension_semantics=("arbitrary",) * n_axes,
                                vmem_limit_bytes=vmem)


def _dot(a, b):
    return jnp.dot(a, b, preferred_element_type=F32)


def _roll4(x):
    return pltpu.roll(x, SUB // 2, 0)


def _sigmoid(x):
    return 0.5 * jnp.tanh(0.5 * x) + 0.5


def _sigmoid2(x):
    return jnp.tanh(0.5 * x) + 1.0


def _lvec(a):
    return a.reshape(DEPTH, 1, a.shape[-1])


def _lspec(l, shape, n_axes=1):
    zeros = (0,) * len(shape)
    if n_axes == 1:
        return pl.BlockSpec((1,) + tuple(shape), lambda i: (l,) + zeros)
    return pl.BlockSpec((1,) + tuple(shape), lambda j, i: (l,) + zeros)


def _s5_prep_kernel(lre_ref, lim_ref, ldt_ref, bre_ref, bim_ref,
                    ar_ref, ai_ref, bbre_ref, bbim_ref):
    lam_re = lre_ref[0]
    lam_im = lim_ref[0]
    dt = jnp.exp(ldt_ref[0])
    mag = jnp.exp(lam_re * dt)
    a_re = mag * jnp.cos(lam_im * dt)
    a_im = mag * jnp.sin(lam_im * dt)
    nr = a_re - 1.0
    ni = a_im
    den = lam_re * lam_re + lam_im * lam_im
    c_re = (nr * lam_re + ni * lam_im) / den
    c_im = (ni * lam_re - nr * lam_im) / den
    b_re = bre_ref[0]
    b_im = bim_ref[0]
    bbre_ref[0] = c_re * b_re - c_im * b_im
    bbim_ref[0] = c_re * b_im + c_im * b_re
    ar_ref[0] = a_re
    ai_ref[0] = a_im


def _s5_prep(lam_re, lam_im, log_dt, bt_re, bt_im):
    vec = pl.BlockSpec((1, 1, S5_N), lambda l: (l, 0, 0))
    mat = pl.BlockSpec((1, S5_GROUP, S5_N), lambda l: (l, 0, 0))
    return pl.pallas_call(
        _s5_prep_kernel,
        grid=(DEPTH,),
        in_specs=[vec, vec, vec, mat, mat],
        out_specs=[vec, vec, mat, mat],
        out_shape=[jax.ShapeDtypeStruct((DEPTH, 1, S5_N), F32)] * 2
        + [jax.ShapeDtypeStruct((DEPTH, S5_GROUP, S5_N), F32)] * 2,
        compiler_params=_cparams(1),
        name="s5_prep",
    )(lam_re, lam_im, log_dt, bt_re, bt_im)


ADA_TN = 1024


def _ada_kernel(c_ref, w_ref, b_ref, o_ref):
    o_ref[0] = _dot(c_ref[...].astype(BF16), w_ref[0].astype(BF16)) + b_ref[0]


def _ada(c_all, w_ada, b_ada):
    rows = c_all.shape[0]
    return pl.pallas_call(
        _ada_kernel,
        grid=(DEPTH, 3 * D_MODEL // ADA_TN),
        in_specs=[pl.BlockSpec((rows, D_MODEL), lambda l, j: (0, 0)),
                  pl.BlockSpec((1, D_MODEL, ADA_TN), lambda l, j: (l, 0, j)),
                  pl.BlockSpec((1, 1, ADA_TN), lambda l, j: (l, 0, j))],
        out_specs=pl.BlockSpec((1, rows, ADA_TN), lambda l, j: (l, 0, j)),
        out_shape=jax.ShapeDtypeStruct((DEPTH, rows, 3 * D_MODEL), F32),
        compiler_params=_cparams(2),
        name="ada",
    )(c_all, w_ada, _lvec(b_ada))


MOD_SHIFT, MOD_SCALE, MOD_GATE = 0, 1, 2


def _mod_spec(l, kind, tm):
    tiles_p = N_P // tm
    return pl.BlockSpec((1, 1, 1, MOD_ROWS, D_MODEL), lambda i: (l, kind, i // tiles_p, 0, 0))


def _rms(x, gain):
    ms = jnp.mean(x * x, axis=-1, keepdims=True)
    return x * lax.rsqrt(ms + EPS) * gain


IO_TM = TM
IO_TT = IO_TM // B_P
IO_TILES_P = N_P // IO_TM

_prompt_io_spec = pl.BlockSpec((B_P, IO_TT, D_MODEL), lambda i: (0, jnp.minimum(i, IO_TILES_P - 1), 0))
_sample_io_spec = pl.BlockSpec((IO_TM, D_MODEL), lambda i: (jnp.maximum(i - IO_TILES_P, 0), 0))


def _embed_kernel(xp_ref, xs_ref, g_ref, sc_ref, sh_ref, xo_ref, xn_ref):
    i = pl.program_id(0)

    @pl.when(i < IO_TILES_P)
    def _prompt():
        xo_ref[...] = jnp.swapaxes(xp_ref[...], 0, 1).reshape(IO_TM, D_MODEL)

    @pl.when(i >= IO_TILES_P)
    def _sample():
        xo_ref[...] = xs_ref[...]

    for s in range(IO_TM // MOD_ROWS):
        rows = slice(s * MOD_ROWS, (s + 1) * MOD_ROWS)
        y = _rms(xo_ref[rows, :], g_ref[0])
        xn_ref[rows, :] = (y * (1.0 + sc_ref[0, 0, 0]) + sh_ref[0, 0, 0]).astype(BF16)


def _embed(x_prompt, x_sample, gain, mods):
    row = pl.BlockSpec((IO_TM, D_MODEL), lambda i: (i, 0))
    return pl.pallas_call(
        _embed_kernel,
        grid=(N_ROWS // IO_TM,),
        in_specs=[_prompt_io_spec, _sample_io_spec, _lspec(0, (1, D_MODEL)),
                  _mod_spec(0, MOD_SCALE, IO_TM), _mod_spec(0, MOD_SHIFT, IO_TM)],
        out_specs=[row, row],
        out_shape=[jax.ShapeDtypeStruct((N_ROWS, D_MODEL), F32),
                   jax.ShapeDtypeStruct((N_ROWS, D_MODEL), BF16)],
        compiler_params=_cparams(1),
        name="embed",
    )(x_prompt, x_sample, gain, mods, mods)


PROJ_TN = 2048
PROJ_TM = N_ROWS // 16
assert N_ROWS % PROJ_TM == 0 and PROJ_TM % (2 * SUB) == 0


def _silu(z):
    hz = 0.5 * z
    return hz * (jnp.tanh(hz) + 1.0)


def _proj_kernel(xn_ref, w_ref, b_ref, o_ref, wbf_ref):
    @pl.when(pl.program_id(1) == 0)
    def _():
        wbf_ref[...] = w_ref[0].astype(BF16)

    o_ref[...] = _dot(xn_ref[...], wbf_ref[...]) + b_ref[0]


def _proj(xn, w_in, b_in, l):
    return pl.pallas_call(
        _proj_kernel,
        grid=(IN_COLS // PROJ_TN, N_ROWS // PROJ_TM),
        in_specs=[pl.BlockSpec((PROJ_TM, D_MODEL), lambda j, i: (i, 0)),
                  pl.BlockSpec((1, D_MODEL, PROJ_TN), lambda j, i: (l, 0, j)),
                  pl.BlockSpec((1, 1, PROJ_TN), lambda j, i: (l, 0, j))],
        out_specs=pl.BlockSpec((PROJ_TM, PROJ_TN), lambda j, i: (i, j)),
        out_shape=jax.ShapeDtypeStruct((N_ROWS, IN_COLS), F32),
        scratch_shapes=[pltpu.VMEM((D_MODEL, PROJ_TN), BF16)],
        compiler_params=_cparams(2),
        name="proj_in",
    )(xn, w_in, b_in)


def _gelu(y):
    return 0.5 * y * (1.0 + lax.erf(y * (2.0 ** -0.5)))


def _s5_kernel(u_ref, z_ref, wbu_ref, wc_ref, ar_ref, ai_ref, d_ref,
               wglu_ref, bglu_ref, h0re_ref, h0im_ref,
               ya_ref, hp_ref, hsre_ref, hsim_ref,
               bu_ref, y_ref, yact_ref, g_ref, p_ref, wglu_bf_ref):
    i = pl.program_id(0)
    half = SUB // 2

    @pl.when(i == 0)
    def _init():
        wglu_bf_ref[...] = wglu_ref[0].astype(BF16)
        p_ref[...] = jnp.zeros_like(p_ref)

    def abar(c):
        lanes = slice(c * S5_CN, (c + 1) * S5_CN)
        a_re = jnp.broadcast_to(ar_ref[0, :, lanes], (SUB, S5_CN))
        a_im = jnp.broadcast_to(ai_ref[0, :, lanes], (SUB, S5_CN))
        return a_re, a_im

    @pl.when(i < NP_TILES)
    def _prompt():
        groups = TM // SUB
        lo3 = lax.broadcasted_iota(jnp.int32, (groups, SUB, S5_CK), 1) < half
        lo = lax.broadcasted_iota(jnp.int32, (SUB, S5_CN), 0) < half
        for c in range(S5_CHUNKS):
            lanes = slice(c * S5_CN, (c + 1) * S5_CN)
            chans = slice(c * S5_CK, (c + 1) * S5_CK)
            buf = bu_ref.at[c % 2]
            v = u_ref[:, chans].reshape(groups, SUB, S5_CK)
            r = pltpu.roll(v, half, 1)
            zero = jnp.zeros_like(v)
            t_even = jnp.concatenate([jnp.where(lo3, v, zero), jnp.where(lo3, zero, r)], axis=-1)
            t_odd = jnp.concatenate([jnp.where(lo3, r, zero), jnp.where(lo3, zero, v)], axis=-1)
            u2 = jnp.stack([t_even, t_odd], axis=1).reshape(2 * TM, 2 * S5_CK)
            buf[...] = _dot(u2.astype(BF16), wbu_ref[0, c])

            a_re, a_im = abar(c)
            a_sw = jnp.where(lo, -a_im, a_im)
            p = p_ref[:, lanes]
            for t in range(TM // B_P):
                rows = slice(t * SUB, (t + 1) * SUB)
                p = a_re * p + a_sw * _roll4(p) + buf[rows, :]
                buf[rows, :] = p
            p_ref[:, lanes] = p

            q = _dot(buf[...].astype(BF16), wc_ref[0, c])
            q = q.reshape(2 * groups, SUB, 2 * S5_CK)
            dd = q[:, :, :S5_CK] - pltpu.roll(q[:, :, S5_CK:], half, 1)
            dd = dd.reshape(groups, 2, SUB, S5_CK)
            yv = jnp.where(lo3, dd[:, 0], pltpu.roll(dd[:, 1], half, 1))
            y_ref[:, chans] = yv.reshape(TM, S5_CK)

    @pl.when(i == NP_TILES)
    def _sample():
        for c in range(S5_CHUNKS):
            lanes = slice(c * S5_CN, (c + 1) * S5_CN)
            chans = slice(c * S5_CK, (c + 1) * S5_CK)
            buf = bu_ref.at[c % 2]
            ub = u_ref[:, chans].astype(BF16)
            buf[0:TM, :] = _dot(ub, wbu_ref[0, c, 0:S5_CK, :])
            buf[TM:2 * TM, :] = _dot(ub, wbu_ref[0, c, S5_CK:2 * S5_CK, :])
            a_re, a_im = abar(c)

            for r0 in range(0, B_S, SUB):
                xr = h0re_ref[0, r0:r0 + SUB, lanes]
                xi = h0im_ref[0, r0:r0 + SUB, lanes]
                for t in range(T_S):
                    rre = slice(t * B_S + r0, t * B_S + r0 + SUB)
                    rim = slice(TM + t * B_S + r0, TM + t * B_S + r0 + SUB)
                    xr, xi = (a_re * xr - a_im * xi + buf[rre, :], a_re * xi + a_im * xr + buf[rim, :])
                    buf[rre, :] = xr
                    buf[rim, :] = xi
                hsre_ref[r0:r0 + SUB, lanes] = xr
                hsim_ref[r0:r0 + SUB, lanes] = xi
            y_ref[:, chans] = (_dot(buf[0:TM, :].astype(BF16), wc_ref[0, c, :, 0:S5_CK])
                               - _dot(buf[TM:2 * TM, :].astype(BF16), wc_ref[0, c, :, S5_CK:2 * S5_CK]))

    @pl.when(i == NP_TILES - 1)
    def _prompt_state():
        hp_ref[...] = p_ref[...]

    rc = 64
    halves = [range(h * (TM // 2), (h + 1) * (TM // 2), rc) for h in range(2)]

    def act(r0):
        rows = slice(r0, r0 + rc)
        a = _gelu(y_ref[rows, :] + d_ref[0] * u_ref[rows, :])
        y_ref[rows, :] = a
        yact_ref[rows, :] = a.astype(BF16)

    def glu(h):
        rows = slice(halves[h][0], halves[h][0] + TM // 2)
        g_ref[rows, :] = _dot(yact_ref[rows, :], wglu_bf_ref[...])

    def gate(r0):
        rows = slice(r0, r0 + rc)
        y = y_ref[rows, :] * _sigmoid(g_ref[rows, :] + bglu_ref[0])
        ya_ref[rows, :] = (y * _silu(z_ref[rows, :])).astype(BF16)

    for r0 in halves[0]:
        act(r0)
    glu(0)
    for r0 in halves[1]:
        act(r0)
    glu(1)
    for r0 in halves[0]:
        gate(r0)
    for r0 in halves[1]:
        gate(r0)


def _s5_branch(proj, wbu, wc, a_re, a_im, d, w_glu, b_glu, h0_re, h0_im, l):
    const2 = lambda i: (0, 0)
    state_s = pl.BlockSpec((B_S, S5_N), const2)
    once = dict(pipeline_mode=pl.Buffered(1))
    lconst = lambda shape: pl.BlockSpec((1,) + shape, lambda i: (l,) + (0,) * len(shape), **once)
    return pl.pallas_call(
        _s5_kernel,
        grid=(N_TILES,),
        in_specs=[pl.BlockSpec((TM, S5_WIDTH), lambda i: (i, 0)),
                  pl.BlockSpec((TM, S5_WIDTH), lambda i: (i, 1)),
                  lconst((S5_CHUNKS, 2 * S5_CK, S5_CN)),
                  lconst((S5_CHUNKS, S5_CN, 2 * S5_CK)),
                  _lspec(l, (1, S5_N)),
                  _lspec(l, (1, S5_N)),
                  _lspec(l, (1, S5_WIDTH)),
                  lconst((S5_WIDTH, S5_WIDTH)),
                  _lspec(l, (1, S5_WIDTH)),
                  lconst((B_S, S5_N)),
                  lconst((B_S, S5_N))],
        out_specs=[pl.BlockSpec((TM, S5_WIDTH), lambda i: (i, 0)),
                   pl.BlockSpec((SUB, S5_N), const2), state_s, state_s],
        out_shape=[jax.ShapeDtypeStruct((N_ROWS, S5_WIDTH), BF16),
                   jax.ShapeDtypeStruct((SUB, S5_N), F32),
                   jax.ShapeDtypeStruct((B_S, S5_N), F32),
                   jax.ShapeDtypeStruct((B_S, S5_N), F32)],
        scratch_shapes=[pltpu.VMEM((2, 2 * TM, S5_CN), F32),
                        pltpu.VMEM((TM, S5_WIDTH), F32),
                        pltpu.VMEM((TM, S5_WIDTH), BF16),
                        pltpu.VMEM((TM, S5_WIDTH), F32),
                        pltpu.VMEM((SUB, S5_N), F32),
                        pltpu.VMEM((S5_WIDTH, S5_WIDTH), BF16)],
        compiler_params=_cparams(1),
        name="s5_branch",
    )(proj, proj, wbu, wc, a_re, a_im, d, w_glu, b_glu, h0_re, h0_im)


RG_LC = 512


RGM_TM = 256
RGM_TILES = N_P // RGM_TM
MERGE_TN = 1024
XB_COL, ZB_COL, GA_COL, GB_COL = 1, 2, 3, 4
assert RG_WIDTH == D_MODEL and 2 * S5_WIDTH == RG_WIDTH


def _conv_taps(cw_ref, cb_ref, lanes):
    taps = [jnp.broadcast_to(cw_ref[0, k:k + 1, lanes], (SUB, RG_LC)) for k in range(CONV_W)]
    return taps, jnp.broadcast_to(cb_ref[0, :, lanes], (SUB, RG_LC))


def _rg_gates(cv_ref, a_ref, wg_ref, br_ref, bi_ref, lam_ref, cvb_ref=None):
    for h in range(RG_BLOCKS):
        lanes = slice(h * RG_BLOCK, (h + 1) * RG_BLOCK)
        sp = lam_ref[0, :, lanes]
        c_lam = -RG_C * (jnp.maximum(-sp, 0.0) + jnp.log1p(jnp.exp(-jnp.abs(sp))))
        k = c_lam * (0.5 * LOG2_E)
        cv = cv_ref[:, lanes]
        cvb = cv.astype(BF16) if cvb_ref is None else cvb_ref[:, lanes]
        g = _dot(cvb, wg_ref[0, h])
        a = jnp.exp2(k * jnp.tanh(0.5 * (g[:, :RG_BLOCK] + br_ref[0, :, lanes])) + k)
        gi = _sigmoid(g[:, RG_BLOCK:] + bi_ref[0, :, lanes])
        m2 = 1.0 - a * a
        mult = jnp.where(m2 > 0.0, m2 * lax.rsqrt(m2), 0.0)
        a_ref[:, lanes] = a
        cv_ref[:, lanes] = mult * gi * cv


def _merge_tile(ya_ref, yb_ref, ga_ref, gb_ref, wa_ref, wb_ref, o_ref, chunks=range(D_MODEL // MERGE_TN)):
    for j in chunks:
        cols = slice(j * MERGE_TN, (j + 1) * MERGE_TN)
        pa = _dot(ya_ref[...], wa_ref[0, :, cols])
        pb = _dot(yb_ref[...], wb_ref[0, :, cols])
        gated = _sigmoid2(ga_ref[:, cols]) * pa + _sigmoid2(gb_ref[:, cols]) * pb
        o_ref[:, cols] = (0.5 * gated).astype(BF16)


def _rgm_prompt_kernel(xb_ref, zb_ref, ya_ref, ga_ref, gb_ref, cw_ref, cb_ref, wg_ref, br_ref,
                       bi_ref, lam_ref, wa_ref, wb_ref,
                       m_ref, hp_ref,
                       cv_ref, cn_ref, a_ref, cvb_ref, yb_cur_ref, yb_prev_ref, hx_ref, pv_ref, pr_ref, ps_ref):
    s = pl.program_id(0)
    groups = RGM_TM // SUB

    @pl.when(s == 0)
    def _init():
        for r in (hx_ref, pv_ref, pr_ref, ps_ref, yb_prev_ref, cv_ref, cvb_ref):
            r[...] = jnp.zeros_like(r)

    chunks = [slice(c * RG_LC, (c + 1) * RG_LC) for c in range(RG_WIDTH // RG_LC)]
    lo = lax.broadcasted_iota(jnp.int32, (SUB, RG_LC), 0) < SUB // 2

    _rg_gates(cv_ref, a_ref, wg_ref, br_ref, bi_ref, lam_ref, cvb_ref)
    _merge_tile(ya_ref, yb_prev_ref, ga_ref, gb_ref, wa_ref, wb_ref, m_ref)

    for lanes in chunks:
        (w0, w1, w2, w3), cb = _conv_taps(cw_ref, cb_ref, lanes)
        v_prev, r_prev, s_prev = pv_ref[:, lanes], pr_ref[:, lanes], ps_ref[:, lanes]
        for k in range(groups):
            rows = slice(k * SUB, (k + 1) * SUB)
            v = xb_ref[rows, lanes]
            r = _roll4(v)
            sh = jnp.where(lo, r_prev, r)
            cn_ref[rows, lanes] = cb + w3 * v + w2 * sh + w1 * v_prev + w0 * s_prev
            v_prev, r_prev, s_prev = v, r, sh
        pv_ref[:, lanes], pr_ref[:, lanes], ps_ref[:, lanes] = v_prev, r_prev, s_prev

    xs = [hx_ref[:, lanes] for lanes in chunks]
    for k in range(groups):
        rows = slice(k * SUB, (k + 1) * SUB)
        for c, lanes in enumerate(chunks):
            a = a_ref[rows, lanes]
            b = cv_ref[rows, lanes]
            y = a * _roll4(xs[c]) + b
            z = a * _roll4(y) + b
            xs[c] = jnp.where(lo, y, z)
            cv_ref[rows, lanes] = xs[c]
    for c, lanes in enumerate(chunks):
        hx_ref[:, lanes] = xs[c]

    rc = 64
    for j in range(RGM_TM // rc):
        rows = slice(j * rc, (j + 1) * rc)
        yb_cur_ref[rows, :] = (cv_ref[rows, :] * _silu(zb_ref[rows, :])).astype(BF16)
    yb_prev_ref[...] = yb_cur_ref[...]
    cv_ref[...] = cn_ref[...]
    cvb_ref[...] = cn_ref[...].astype(BF16)

    @pl.when(s == RGM_TILES)
    def _prompt_state():
        hp_ref[...] = hx_ref[...]


def _rgm_prompt(proj, ya, conv_w, conv_b, wg, b_r, b_i, lam, w_a, w_b, l):
    vec = _lspec(l, (1, RG_WIDTH))
    last = RGM_TILES - 1
    lag = lambda w, col, d: pl.BlockSpec((RGM_TM, w), lambda s: (jnp.clip(s - d, 0, last), col))
    once = dict(pipeline_mode=pl.Buffered(1))
    carry = pltpu.VMEM((SUB, RG_WIDTH), F32)
    return pl.pallas_call(
        _rgm_prompt_kernel,
        grid=(RGM_TILES + 2,),
        in_specs=[lag(RG_WIDTH, XB_COL, 0), lag(RG_WIDTH, ZB_COL, 1), lag(S5_WIDTH, 0, 2),
                  lag(D_MODEL, GA_COL, 2), lag(D_MODEL, GB_COL, 2),
                  _lspec(l, (CONV_W, RG_WIDTH)), vec,
                  _lspec(l, (RG_BLOCKS, RG_BLOCK, 2 * RG_BLOCK)), vec, vec, vec,
                  pl.BlockSpec((1, S5_WIDTH, D_MODEL), lambda s: (l, 0, 0), **once),
                  pl.BlockSpec((1, RG_WIDTH, D_MODEL), lambda s: (l, 0, 0), **once)],
        out_specs=[lag(D_MODEL, 0, 2), pl.BlockSpec((SUB, RG_WIDTH), lambda s: (0, 0))],
        out_shape=[jax.ShapeDtypeStruct((N_P, D_MODEL), BF16),
                   jax.ShapeDtypeStruct((SUB, RG_WIDTH), F32)],
        scratch_shapes=[pltpu.VMEM((RGM_TM, RG_WIDTH), F32),
                        pltpu.VMEM((RGM_TM, RG_WIDTH), F32),
                        pltpu.VMEM((RGM_TM, RG_WIDTH), F32),
                        pltpu.VMEM((RGM_TM, RG_WIDTH), BF16),
                        pltpu.VMEM((RGM_TM, RG_WIDTH), BF16),
                        pltpu.VMEM((RGM_TM, RG_WIDTH), BF16),
                        carry, carry, carry, carry],
        compiler_params=_cparams(1),
        name="rgm_prompt",
    )(proj, proj, ya, proj, proj, conv_w, conv_b, wg, b_r, b_i, lam, w_a, w_b)


def _rgm_sample_kernel(xb_ref, zb_ref, ya_ref, ga_ref, gb_ref, cw_ref, cb_ref, wg_ref,
                       br_ref, bi_ref, lam_ref, wa_ref, wb_ref, h0_ref, c0_ref,
                       m_ref, hs_ref,
                       cv_ref, a_ref, yb_ref):
    def src(t, r0, lanes):
        if t >= 0:
            return xb_ref[pl.ds(t * B_S + r0, SUB), lanes]
        return c0_ref[0, pl.ds((CONV_W - 1 + t) * B_S + r0, SUB), lanes]

    for c in range(RG_WIDTH // RG_LC):
        lanes = slice(c * RG_LC, (c + 1) * RG_LC)
        (w0, w1, w2, w3), cb = _conv_taps(cw_ref, cb_ref, lanes)

        def body(g, carry, w0=w0, w1=w1, w2=w2, w3=w3, cb=cb, lanes=lanes):
            r0 = pl.multiple_of(g * SUB, SUB)
            for t in range(T_S):
                cv_ref[pl.ds(t * B_S + r0, SUB), lanes] = (
                    cb + w3 * src(t, r0, lanes) + w2 * src(t - 1, r0, lanes)
                    + w1 * src(t - 2, r0, lanes) + w0 * src(t - 3, r0, lanes))
            return carry

        lax.fori_loop(0, B_S // SUB, body, 0)

    _rg_gates(cv_ref, a_ref, wg_ref, br_ref, bi_ref, lam_ref)

    for c in range(RG_WIDTH // RG_LC):
        lanes = slice(c * RG_LC, (c + 1) * RG_LC)

        def body(g, carry, lanes=lanes):
            r0 = pl.multiple_of(g * SUB, SUB)
            x = h0_ref[0, pl.ds(r0, SUB), lanes]
            for t in range(T_S):
                rows = pl.ds(t * B_S + r0, SUB)
                x = a_ref[rows, lanes] * x + cv_ref[rows, lanes]
                cv_ref[rows, lanes] = x
            hs_ref[pl.ds(r0, SUB), lanes] = x
            return carry

        lax.fori_loop(0, B_S // SUB, body, 0)

    rc = 64

    def out(j, carry):
        rows = pl.ds(pl.multiple_of(j * rc, rc), rc)
        yb_ref[rows, :] = (cv_ref[rows, :] * _silu(zb_ref[rows, :])).astype(BF16)
        return carry

    lax.fori_loop(0, N_S // rc, out, 0)
    _merge_tile(ya_ref, yb_ref, ga_ref, gb_ref, wa_ref, wb_ref, m_ref)


def _rgm_sample(proj, ya, conv_w, conv_b, wg, b_r, b_i, lam, w_a, w_b, h0, conv0, l):
    once = dict(pipeline_mode=pl.Buffered(1))
    tile = N_P // N_S
    rows = lambda w, col: pl.BlockSpec((N_S, w), lambda i: (tile, col), **once)
    lconst = lambda shape: pl.BlockSpec((1,) + shape, lambda i: (l,) + (0,) * len(shape), **once)
    vec = lconst((1, RG_WIDTH))
    return pl.pallas_call(
        _rgm_sample_kernel,
        grid=(1,),
        in_specs=[rows(RG_WIDTH, XB_COL), rows(RG_WIDTH, ZB_COL), rows(S5_WIDTH, 0),
                  rows(D_MODEL, GA_COL), rows(D_MODEL, GB_COL),
                  lconst((CONV_W, RG_WIDTH)), vec,
                  lconst((RG_BLOCKS, RG_BLOCK, 2 * RG_BLOCK)), vec, vec, vec,
                  lconst((S5_WIDTH, D_MODEL)), lconst((RG_WIDTH, D_MODEL)),
                  lconst((B_S, RG_WIDTH)), lconst(((CONV_W - 1) * B_S, RG_WIDTH))],
        out_specs=[pl.BlockSpec((N_S, D_MODEL), lambda i: (0, 0)),
                   pl.BlockSpec((B_S, RG_WIDTH), lambda i: (0, 0))],
        out_shape=[jax.ShapeDtypeStruct((N_S, D_MODEL), BF16),
                   jax.ShapeDtypeStruct((B_S, RG_WIDTH), F32)],
        scratch_shapes=[pltpu.VMEM((N_S, RG_WIDTH), F32),
                        pltpu.VMEM((N_S, RG_WIDTH), F32),
                        pltpu.VMEM((N_S, RG_WIDTH), BF16)],
        compiler_params=_cparams(1),
        name="rgm_sample",
    )(proj, proj, ya, proj, proj, conv_w, conv_b, wg, b_r, b_i, lam, w_a, w_b, h0, conv0)


OUT_TM = IO_TM


def _out_kernel(last, mp_ref, ms_ref, x_ref, w_ref, gate_ref, g_ref, *rest):
    if last:
        yp_ref, ys_ref = rest
    else:
        sc_ref, sh_ref, xo_ref, xn_ref = rest
    i = pl.program_id(0)
    tt = MOD_ROWS // B_P

    m = jnp.where(i < IO_TILES_P, mp_ref[...], ms_ref[...])
    out = _dot(m, w_ref[0])
    for s in range(OUT_TM // MOD_ROWS):
        rows = slice(s * MOD_ROWS, (s + 1) * MOD_ROWS)
        x = x_ref[rows, :] + gate_ref[0, 0, 0] * out[rows, :]
        y = _rms(x, g_ref[0])
        if last:
            @pl.when(i < IO_TILES_P)
            def _prompt(y=y, s=s):
                yp_ref[:, s * tt:(s + 1) * tt, :] = jnp.swapaxes(y.reshape(tt, B_P, D_MODEL), 0, 1)

            @pl.when(i >= IO_TILES_P)
            def _sample(y=y, rows=rows):
                ys_ref[rows, :] = y
        else:
            xo_ref[rows, :] = x
            xn_ref[rows, :] = (y * (1.0 + sc_ref[0, 0, 0]) + sh_ref[0, 0, 0]).astype(BF16)


def _out_layer(merged_p, merged_s, x, w_out, mods, l, gain, gain_l):
    last = l + 1 == DEPTH
    row = pl.BlockSpec((OUT_TM, D_MODEL), lambda i: (i, 0))
    row_p = pl.BlockSpec((OUT_TM, D_MODEL), lambda i: (jnp.minimum(i, IO_TILES_P - 1), 0))
    row_s = pl.BlockSpec((OUT_TM, D_MODEL), lambda i: (jnp.maximum(i - IO_TILES_P, 0), 0),
                         pipeline_mode=pl.Buffered(1))
    in_specs = [row_p, row_s, row,
                pl.BlockSpec((1, D_MODEL, D_MODEL), lambda i: (l, 0, 0),
                             pipeline_mode=pl.Buffered(1)),
                _mod_spec(l, MOD_GATE, OUT_TM),
                _lspec(gain_l, (1, D_MODEL))]
    args = [merged_p, merged_s, x, w_out, mods, gain]
    if last:
        out_specs = [_prompt_io_spec, _sample_io_spec]
        out_shape = [jax.ShapeDtypeStruct((B_P, T_P, D_MODEL), F32),
                     jax.ShapeDtypeStruct((N_S, D_MODEL), F32)]
    else:
        in_specs += [_mod_spec(l + 1, MOD_SCALE, OUT_TM), _mod_spec(l + 1, MOD_SHIFT, OUT_TM)]
        args += [mods, mods]
        out_specs = [row, row]
        out_shape = [jax.ShapeDtypeStruct((N_ROWS, D_MODEL), F32),
                     jax.ShapeDtypeStruct((N_ROWS, D_MODEL), BF16)]
    return pl.pallas_call(
        functools.partial(_out_kernel, last),
        grid=(N_ROWS // OUT_TM,),
        in_specs=in_specs,
        out_specs=out_specs,
        out_shape=out_shape,
        compiler_params=_cparams(1),
        name="out_last" if last else "out_layer",
    )(*args)


def _batch_major(rows, b, t):
    lead = rows.shape[:-2]
    return jnp.swapaxes(rows.reshape(lead + (t, b, rows.shape[-1])), -3, -2)


def _block_diag_in(bbt):
    gpc = S5_GROUPS // S5_CHUNKS
    w = bbt.reshape(DEPTH, S5_GROUP, S5_CHUNKS, gpc, S5_STATE)
    eye = jnp.eye(gpc, dtype=bbt.dtype)
    w = jnp.einsum("lkcgp,gh->lcgkhp", w, eye)
    return w.reshape(DEPTH, S5_CHUNKS, S5_CK, S5_CN)


def _block_diag_out(cmat):
    gpc = S5_GROUPS // S5_CHUNKS
    w = cmat.reshape(DEPTH, S5_CHUNKS, gpc, S5_GROUP, S5_STATE)
    eye = jnp.eye(gpc, dtype=cmat.dtype)
    w = jnp.einsum("lcgkp,gh->lchpgk", w, eye)
    return w.reshape(DEPTH, S5_CHUNKS, S5_CN, S5_CK)


def _mod_patterns(ada):
    a = ada.reshape(DEPTH, ada.shape[1], 3, D_MODEL).transpose(0, 2, 1, 3)
    p = jnp.tile(a[:, :, :B_P], (1, 1, MOD_ROWS // B_P, 1))
    s = a[:, :, B_P:B_P + B_S]
    return jnp.stack([p, s], axis=2)


def kernel(x_prompt, x_sample, state_s5_re, state_s5_im, state_rglru_h, state_conv,
           c_prompt, c_sample, w_ada, b_ada, norm_gain, w_in, b_in,
           s5_lam_re, s5_lam_im, s5_log_dt, s5_b_re, s5_b_im, s5_c_re, s5_c_im,
           s5_d, s5_w_glu, s5_b_glu, rg_conv_w, rg_conv_b, rg_w_r, rg_b_r, rg_w_i, rg_b_i,
           rg_lam, w_proj_a, w_proj_b, w_out, final_gain):
    sdt = state_s5_re.dtype

    to_lanes = lambda v: v.reshape(DEPTH, 1, S5_N)
    bt = lambda b: b.reshape(DEPTH, S5_N, S5_GROUP).transpose(0, 2, 1)
    log_dt = jnp.repeat(s5_log_dt, S5_STATE, axis=-1)
    a_re, a_im, bbt_re, bbt_im = _s5_prep(to_lanes(s5_lam_re), to_lanes(s5_lam_im),
                                          to_lanes(log_dt), bt(s5_b_re), bt(s5_b_im))
    wbu = jnp.concatenate([_block_diag_in(bbt_re), _block_diag_in(bbt_im)], axis=-2).astype(BF16)
    wc = jnp.concatenate([_block_diag_out(s5_c_re), _block_diag_out(s5_c_im)], axis=-1).astype(BF16)
    wg = jnp.concatenate([rg_w_r, rg_w_i], axis=-1).astype(BF16)
    wa_bf, wb_bf, wo_bf = w_proj_a.astype(BF16), w_proj_b.astype(BF16), w_out.astype(BF16)

    n_c = B_P + B_S
    c_all = jnp.concatenate([c_prompt, c_sample, jnp.zeros((-n_c % SUB, D_MODEL), F32)], axis=0)
    mods = _mod_patterns(_ada(c_all, w_ada, b_ada))

    h0_re = state_s5_re.reshape(DEPTH, B_S, S5_N)
    h0_im = state_s5_im.reshape(DEPTH, B_S, S5_N)
    conv0 = state_conv.transpose(0, 2, 1, 3).reshape(DEPTH, (CONV_W - 1) * B_S, RG_WIDTH)
    gains = jnp.concatenate([norm_gain, final_gain[None]], axis=0).reshape(DEPTH + 1, 1, D_MODEL)

    xs_tm = x_sample.transpose(1, 0, 2).reshape(N_S, D_MODEL)
    x, xn = _embed(x_prompt, xs_tm, gains, mods)

    outs = {k: [] for k in ("s5re_p", "s5im_p", "h_p", "conv_p", "s5re_s", "s5im_s", "h_s", "conv_s")}
    y_prompt = y_sample = None
    half = SUB // 2
    xb_cols = slice(2 * S5_WIDTH, 2 * S5_WIDTH + RG_WIDTH)
    tail = CONV_W - 1
    for l in range(DEPTH):
        proj = _proj(xn, w_in, _lvec(b_in), l)
        ya, hp_s5, hs_re, hs_im = _s5_branch(
            proj, wbu, wc, a_re, a_im, _lvec(s5_d), s5_w_glu, _lvec(s5_b_glu), h0_re, h0_im, l)
        rg_params = (rg_conv_w, _lvec(rg_conv_b), wg, _lvec(rg_b_r), _lvec(rg_b_i), _lvec(rg_lam),
                     wa_bf, wb_bf)
        merged_p, hp = _rgm_prompt(proj, ya, *rg_params, l)
        merged_s, hs = _rgm_sample(proj, ya, *rg_params, state_rglru_h, conv0, l)
        res = _out_layer(merged_p, merged_s, x, wo_bf, mods, l, gains, l + 1)
        if l + 1 < DEPTH:
            x, xn = res
        else:
            y_prompt, y_sample = res[0], _batch_major(res[1], B_S, T_S)

        outs["s5re_p"].append(hp_s5[:half])
        outs["s5im_p"].append(hp_s5[half:])
        outs["h_p"].append(hp[half:])
        outs["conv_p"].append(proj[N_P - tail * B_P:N_P, xb_cols])
        outs["s5re_s"].append(hs_re)
        outs["s5im_s"].append(hs_im)
        outs["h_s"].append(hs)
        outs["conv_s"].append(proj[N_ROWS - tail * B_S:, xb_cols])

    st = lambda k: jnp.stack(outs[k]).astype(sdt)
    s5_shape = lambda b: (DEPTH, b, S5_GROUPS, S5_STATE)
    return (y_prompt, y_sample,
            st("s5re_p").reshape(s5_shape(B_P)), st("s5im_p").reshape(s5_shape(B_P)),
            st("h_p"), _batch_major(st("conv_p"), B_P, tail),
            st("s5re_s").reshape(s5_shape(B_S)), st("s5im_s").reshape(s5_shape(B_S)),
            st("h_s"), _batch_major(st("conv_s"), B_S, tail))
```

```python
import functools

import jax
import jax.numpy as jnp
from jax import lax
from jax.experimental import pallas as pl
from jax.experimental.pallas import tpu as pltpu

F32 = jnp.float32
BF16 = jnp.bfloat16

D_MODEL = 2048
DEPTH = 4
B_P, T_P = 4, 2048
B_S, T_S = 128, 4
S5_WIDTH = 1024
S5_GROUP = 16
S5_GROUPS = 64
S5_STATE = 64
S5_N = S5_GROUPS * S5_STATE
RG_WIDTH = 2048
RG_BLOCKS = 16
RG_BLOCK = 128
RG_C = 8.0
CONV_W = 4
IN_COLS = 2 * S5_WIDTH + 2 * RG_WIDTH + 2 * D_MODEL
EPS = 1e-6
LOG2_E = 1.4426950408889634

TM = 512
N_P = B_P * T_P
N_S = B_S * T_S
N_ROWS = N_P + N_S
NP_TILES = N_P // TM
N_TILES = N_ROWS // TM
MOD_ROWS = 128
S5_CHUNKS = 8
S5_CK = S5_WIDTH // S5_CHUNKS
S5_CN = S5_N // S5_CHUNKS
SUB = 8
VMEM_LIMIT = 56 * 1024 * 1024

assert N_S == TM and N_P % TM == 0 and TM % MOD_ROWS == 0


def _cparams(n_axes, vmem=VMEM_LIMIT):
    return pltpu.CompilerParams(dimension_semantics=("arbitrary",) * n_axes,
                                vmem_limit_bytes=vmem)


def _dot(a, b):
    return jnp.dot(a, b, preferred_element_type=F32)


def _roll4(x):
    return pltpu.roll(x, SUB // 2, 0)


def _sigmoid(x):
    return 0.5 * jnp.tanh(0.5 * x) + 0.5


def _sigmoid2(x):
    return jnp.tanh(0.5 * x) + 1.0


def _lvec(a):
    return a.reshape(DEPTH, 1, a.shape[-1])


def _lspec(l, shape, n_axes=1):
    zeros = (0,) * len(shape)
    if n_axes == 1:
        return pl.BlockSpec((1,) + tuple(shape), lambda i: (l,) + zeros)
    return pl.BlockSpec((1,) + tuple(shape), lambda j, i: (l,) + zeros)


def _s5_prep_kernel(lre_ref, lim_ref, ldt_ref, bre_ref, bim_ref,
                    ar_ref, ai_ref, bbre_ref, bbim_ref):
    lam_re = lre_ref[0]
    lam_im = lim_ref[0]
    dt = jnp.exp(ldt_ref[0])
    mag = jnp.exp(lam_re * dt)
    a_re = mag * jnp.cos(lam_im * dt)
    a_im = mag * jnp.sin(lam_im * dt)
    nr = a_re - 1.0
    ni = a_im
    den = lam_re * lam_re + lam_im * lam_im
    c_re = (nr * lam_re + ni * lam_im) / den
    c_im = (ni * lam_re - nr * lam_im) / den
    b_re = bre_ref[0]
    b_im = bim_ref[0]
    bbre_ref[0] = c_re * b_re - c_im * b_im
    bbim_ref[0] = c_re * b_im + c_im * b_re
    ar_ref[0] = a_re
    ai_ref[0] = a_im


def _s5_prep(lam_re, lam_im, log_dt, bt_re, bt_im):
    vec = pl.BlockSpec((1, 1, S5_N), lambda l: (l, 0, 0))
    mat = pl.BlockSpec((1, S5_GROUP, S5_N), lambda l: (l, 0, 0))
    return pl.pallas_call(
        _s5_prep_kernel,
        grid=(DEPTH,),
        in_specs=[vec, vec, vec, mat, mat],
        out_specs=[vec, vec, mat, mat],
        out_shape=[jax.ShapeDtypeStruct((DEPTH, 1, S5_N), F32)] * 2
        + [jax.ShapeDtypeStruct((DEPTH, S5_GROUP, S5_N), F32)] * 2,
        compiler_params=_cparams(1),
        name="s5_prep",
    )(lam_re, lam_im, log_dt, bt_re, bt_im)


ADA_TN = 1024


def _ada_kernel(c_ref, w_ref, b_ref, o_ref):
    o_ref[0] = _dot(c_ref[...].astype(BF16), w_ref[0].astype(BF16)) + b_ref[0]


def _ada(c_all, w_ada, b_ada):
    rows = c_all.shape[0]
    return pl.pallas_call(
        _ada_kernel,
        grid=(DEPTH, 3 * D_MODEL // ADA_TN),
        in_specs=[pl.BlockSpec((rows, D_MODEL), lambda l, j: (0, 0)),
                  pl.BlockSpec((1, D_MODEL, ADA_TN), lambda l, j: (l, 0, j)),
                  pl.BlockSpec((1, 1, ADA_TN), lambda l, j: (l, 0, j))],
        out_specs=pl.BlockSpec((1, rows, ADA_TN), lambda l, j: (l, 0, j)),
        out_shape=jax.ShapeDtypeStruct((DEPTH, rows, 3 * D_MODEL), F32),
        compiler_params=_cparams(2),
        name="ada",
    )(c_all, w_ada, _lvec(b_ada))


MOD_SHIFT, MOD_SCALE, MOD_GATE = 0, 1, 2


def _mod_spec(l, kind, tm):
    tiles_p = N_P // tm
    return pl.BlockSpec((1, 1, 1, MOD_ROWS, D_MODEL), lambda i: (l, kind, i // tiles_p, 0, 0))


def _rms(x, gain):
    ms = jnp.mean(x * x, axis=-1, keepdims=True)
    return x * lax.rsqrt(ms + EPS) * gain


IO_TM = TM
IO_TT = IO_TM // B_P
IO_TILES_P = N_P // IO_TM

_prompt_io_spec = pl.BlockSpec((B_P, IO_TT, D_MODEL), lambda i: (0, jnp.minimum(i, IO_TILES_P - 1), 0))
_sample_io_spec = pl.BlockSpec((IO_TM, D_MODEL), lambda i: (jnp.maximum(i - IO_TILES_P, 0), 0))


def _embed_kernel(xp_ref, xs_ref, g_ref, sc_ref, sh_ref, xo_ref, xn_ref):
    i = pl.program_id(0)

    @pl.when(i < IO_TILES_P)
    def _prompt():
        xo_ref[...] = jnp.swapaxes(xp_ref[...], 0, 1).reshape(IO_TM, D_MODEL)

    @pl.when(i >= IO_TILES_P)
    def _sample():
        xo_ref[...] = xs_ref[...]

    for s in range(IO_TM // MOD_ROWS):
        rows = slice(s * MOD_ROWS, (s + 1) * MOD_ROWS)
        y = _rms(xo_ref[rows, :], g_ref[0])
        xn_ref[rows, :] = (y * (1.0 + sc_ref[0, 0, 0]) + sh_ref[0, 0, 0]).astype(BF16)


def _embed(x_prompt, x_sample, gain, mods):
    row = pl.BlockSpec((IO_TM, D_MODEL), lambda i: (i, 0))
    return pl.pallas_call(
        _embed_kernel,
        grid=(N_ROWS // IO_TM,),
        in_specs=[_prompt_io_spec, _sample_io_spec, _lspec(0, (1, D_MODEL)),
                  _mod_spec(0, MOD_SCALE, IO_TM), _mod_spec(0, MOD_SHIFT, IO_TM)],
        out_specs=[row, row],
        out_shape=[jax.ShapeDtypeStruct((N_ROWS, D_MODEL), F32),
                   jax.ShapeDtypeStruct((N_ROWS, D_MODEL), BF16)],
        compiler_params=_cparams(1),
        name="embed",
    )(x_prompt, x_sample, gain, mods, mods)


PROJ_TN = 2048
PROJ_TM = N_ROWS // 16
assert N_ROWS % PROJ_TM == 0 and PROJ_TM % (2 * SUB) == 0


def _silu(z):
    hz = 0.5 * z
    return hz * (jnp.tanh(hz) + 1.0)


def _proj_kernel(xn_ref, w_ref, b_ref, o_ref, wbf_ref):
    @pl.when(pl.program_id(1) == 0)
    def _():
        wbf_ref[...] = w_ref[0].astype(BF16)

    o_ref[...] = _dot(xn_ref[...], wbf_ref[...]) + b_ref[0]


def _proj(xn, w_in, b_in, l):
    return pl.pallas_call(
        _proj_kernel,
        grid=(IN_COLS // PROJ_TN, N_ROWS // PROJ_TM),
        in_specs=[pl.BlockSpec((PROJ_TM, D_MODEL), lambda j, i: (i, 0)),
                  pl.BlockSpec((1, D_MODEL, PROJ_TN), lambda j, i: (l, 0, j)),
                  pl.BlockSpec((1, 1, PROJ_TN), lambda j, i: (l, 0, j))],
        out_specs=pl.BlockSpec((PROJ_TM, PROJ_TN), lambda j, i: (i, j)),
        out_shape=jax.ShapeDtypeStruct((N_ROWS, IN_COLS), F32),
        scratch_shapes=[pltpu.VMEM((D_MODEL, PROJ_TN), BF16)],
        compiler_params=_cparams(2),
        name="proj_in",
    )(xn, w_in, b_in)


def _gelu(y):
    return 0.5 * y * (1.0 + lax.erf(y * (2.0 ** -0.5)))


def _s5_kernel(u_ref, z_ref, wbu_ref, wc_ref, ar_ref, ai_ref, d_ref,
               wglu_ref, bglu_ref, h0re_ref, h0im_ref,
               ya_ref, hp_ref, hsre_ref, hsim_ref,
               bu_ref, hs_ref, y_ref, yact_ref, g_ref, p_ref, wglu_bf_ref):
    i = pl.program_id(0)
    half = SUB // 2

    @pl.when(i == 0)
    def _init():
        wglu_bf_ref[...] = wglu_ref[0].astype(BF16)
        p_ref[...] = jnp.zeros_like(p_ref)

    def abar(c):
        lanes = slice(c * S5_CN, (c + 1) * S5_CN)
        a_re = jnp.broadcast_to(ar_ref[0, :, lanes], (SUB, S5_CN))
        a_im = jnp.broadcast_to(ai_ref[0, :, lanes], (SUB, S5_CN))
        return a_re, a_im

    @pl.when(i < NP_TILES)
    def _prompt():
        groups = TM // SUB
        lo3 = lax.broadcasted_iota(jnp.int32, (groups, SUB, S5_CK), 1) < half
        lo = lax.broadcasted_iota(jnp.int32, (SUB, S5_CN), 0) < half
        for c in range(S5_CHUNKS):
            lanes = slice(c * S5_CN, (c + 1) * S5_CN)
            chans = slice(c * S5_CK, (c + 1) * S5_CK)
            buf = bu_ref.at[c % 2]
            hs = hs_ref.at[c % 2]
            v = u_ref[:, chans].reshape(groups, SUB, S5_CK)
            r = pltpu.roll(v, half, 1)
            zero = jnp.zeros_like(v)
            t_even = jnp.concatenate([jnp.where(lo3, v, zero), jnp.where(lo3, zero, r)], axis=-1)
            t_odd = jnp.concatenate([jnp.where(lo3, r, zero), jnp.where(lo3, zero, v)], axis=-1)
            u2 = jnp.stack([t_even, t_odd], axis=1).reshape(2 * TM, 2 * S5_CK)
            buf[...] = _dot(u2.astype(BF16), wbu_ref[0, c])

            a_re, a_im = abar(c)
            a_sw = jnp.where(lo, -a_im, a_im)
            p = p_ref[:, lanes]
            for t in range(TM // B_P):
                rows = slice(t * SUB, (t + 1) * SUB)
                p = a_re * p + a_sw * _roll4(p) + buf[rows, :]
                hs[rows, :] = p
            p_ref[:, lanes] = p

            q = _dot(hs[...].astype(BF16), wc_ref[0, c])
            q = q.reshape(2 * groups, SUB, 2 * S5_CK)
            dd = q[:, :, :S5_CK] - pltpu.roll(q[:, :, S5_CK:], half, 1)
            dd = dd.reshape(groups, 2, SUB, S5_CK)
            yv = jnp.where(lo3, dd[:, 0], pltpu.roll(dd[:, 1], half, 1))
            y_ref[:, chans] = yv.reshape(TM, S5_CK)

    @pl.when(i == NP_TILES)
    def _sample():
        for c in range(S5_CHUNKS):
            lanes = slice(c * S5_CN, (c + 1) * S5_CN)
            chans = slice(c * S5_CK, (c + 1) * S5_CK)
            buf = bu_ref.at[c % 2]
            ub = u_ref[:, chans].astype(BF16)
            buf[0:TM, :] = _dot(ub, wbu_ref[0, c, 0:S5_CK, :])
            buf[TM:2 * TM, :] = _dot(ub, wbu_ref[0, c, S5_CK:2 * S5_CK, :])
            a_re, a_im = abar(c)

            for r0 in range(0, B_S, SUB):
                xr = h0re_ref[0, r0:r0 + SUB, lanes]
                xi = h0im_ref[0, r0:r0 + SUB, lanes]
                for t in range(T_S):
                    rre = slice(t * B_S + r0, t * B_S + r0 + SUB)
                    rim = slice(TM + t * B_S + r0, TM + t * B_S + r0 + SUB)
                    xr, xi = (a_re * xr - a_im * xi + buf[rre, :], a_re * xi + a_im * xr + buf[rim, :])
                    buf[rre, :] = xr
                    buf[rim, :] = xi
                hsre_ref[r0:r0 + SUB, lanes] = xr
                hsim_ref[r0:r0 + SUB, lanes] = xi
            y_ref[:, chans] = (_dot(buf[0:TM, :].astype(BF16), wc_ref[0, c, :, 0:S5_CK])
                               - _dot(buf[TM:2 * TM, :].astype(BF16), wc_ref[0, c, :, S5_CK:2 * S5_CK]))

    @pl.when(i == NP_TILES - 1)
    def _prompt_state():
        hp_ref[...] = p_ref[...]

    rc = 64
    halves = [range(h * (TM // 2), (h + 1) * (TM // 2), rc) for h in range(2)]

    def act(r0):
        rows = slice(r0, r0 + rc)
        a = _gelu(y_ref[rows, :] + d_ref[0] * u_ref[rows, :])
        y_ref[rows, :] = a
        yact_ref[rows, :] = a.astype(BF16)

    def glu(h):
        rows = slice(halves[h][0], halves[h][0] + TM // 2)
        g_ref[rows, :] = _dot(yact_ref[rows, :], wglu_bf_ref[...])

    def gate(r0):
        rows = slice(r0, r0 + rc)
        y = y_ref[rows, :] * _sigmoid(g_ref[rows, :] + bglu_ref[0])
        ya_ref[rows, :] = (y * _silu(z_ref[rows, :])).astype(BF16)

    for r0 in halves[0]:
        act(r0)
    glu(0)
    for r0 in halves[1]:
        act(r0)
    glu(1)
    for r0 in halves[0]:
        gate(r0)
    for r0 in halves[1]:
        gate(r0)


def _s5_branch(proj, wbu, wc, a_re, a_im, d, w_glu, b_glu, h0_re, h0_im, l):
    const2 = lambda i: (0, 0)
    state_s = pl.BlockSpec((B_S, S5_N), const2)
    once = dict(pipeline_mode=pl.Buffered(1))
    lconst = lambda shape: pl.BlockSpec((1,) + shape, lambda i: (l,) + (0,) * len(shape), **once)
    return pl.pallas_call(
        _s5_kernel,
        grid=(N_TILES,),
        in_specs=[pl.BlockSpec((TM, S5_WIDTH), lambda i: (i, 0)),
                  pl.BlockSpec((TM, S5_WIDTH), lambda i: (i, 1)),
                  lconst((S5_CHUNKS, 2 * S5_CK, S5_CN)),
                  lconst((S5_CHUNKS, S5_CN, 2 * S5_CK)),
                  _lspec(l, (1, S5_N)),
                  _lspec(l, (1, S5_N)),
                  _lspec(l, (1, S5_WIDTH)),
                  lconst((S5_WIDTH, S5_WIDTH)),
                  _lspec(l, (1, S5_WIDTH)),
                  lconst((B_S, S5_N)),
                  lconst((B_S, S5_N))],
        out_specs=[pl.BlockSpec((TM, S5_WIDTH), lambda i: (i, 0)),
                   pl.BlockSpec((SUB, S5_N), const2), state_s, state_s],
        out_shape=[jax.ShapeDtypeStruct((N_ROWS, S5_WIDTH), BF16),
                   jax.ShapeDtypeStruct((SUB, S5_N), F32),
                   jax.ShapeDtypeStruct((B_S, S5_N), F32),
                   jax.ShapeDtypeStruct((B_S, S5_N), F32)],
        scratch_shapes=[pltpu.VMEM((2, 2 * TM, S5_CN), F32),
                        pltpu.VMEM((2, 2 * TM, S5_CN), F32),
                        pltpu.VMEM((TM, S5_WIDTH), F32),
                        pltpu.VMEM((TM, S5_WIDTH), BF16),
                        pltpu.VMEM((TM, S5_WIDTH), F32),
                        pltpu.VMEM((SUB, S5_N), F32),
                        pltpu.VMEM((S5_WIDTH, S5_WIDTH), BF16)],
        compiler_params=_cparams(1),
        name="s5_branch",
    )(proj, proj, wbu, wc, a_re, a_im, d, w_glu, b_glu, h0_re, h0_im)


RG_LC = 512


RGM_TM = 256
RGM_TILES = N_P // RGM_TM
MERGE_TN = 1024
XB_COL, ZB_COL, GA_COL, GB_COL = 1, 2, 3, 4
assert RG_WIDTH == D_MODEL and 2 * S5_WIDTH == RG_WIDTH


def _conv_taps(cw_ref, cb_ref, lanes):
    taps = [jnp.broadcast_to(cw_ref[0, k:k + 1, lanes], (SUB, RG_LC)) for k in range(CONV_W)]
    return taps, jnp.broadcast_to(cb_ref[0, :, lanes], (SUB, RG_LC))


def _rg_gates(cv_ref, a_ref, wg_ref, br_ref, bi_ref, lam_ref, cvb_ref=None):
    for h in range(RG_BLOCKS):
        lanes = slice(h * RG_BLOCK, (h + 1) * RG_BLOCK)
        sp = lam_ref[0, :, lanes]
        c_lam = -RG_C * (jnp.maximum(-sp, 0.0) + jnp.log1p(jnp.exp(-jnp.abs(sp))))
        k = c_lam * (0.5 * LOG2_E)
        cv = cv_ref[:, lanes]
        cvb = cv.astype(BF16) if cvb_ref is None else cvb_ref[:, lanes]
        g = _dot(cvb, wg_ref[0, h])
        a = jnp.exp2(k * jnp.tanh(0.5 * (g[:, :RG_BLOCK] + br_ref[0, :, lanes])) + k)
        gi = _sigmoid(g[:, RG_BLOCK:] + bi_ref[0, :, lanes])
        m2 = 1.0 - a * a
        mult = jnp.where(m2 > 0.0, m2 * lax.rsqrt(m2), 0.0)
        a_ref[:, lanes] = a
        cv_ref[:, lanes] = mult * gi * cv


def _merge_tile(ya_ref, yb_ref, ga_ref, gb_ref, wa_ref, wb_ref, o_ref, chunks=range(D_MODEL // MERGE_TN)):
    for j in chunks:
        cols = slice(j * MERGE_TN, (j + 1) * MERGE_TN)
        pa = _dot(ya_ref[...], wa_ref[0, :, cols])
        pb = _dot(yb_ref[...], wb_ref[0, :, cols])
        gated = _sigmoid2(ga_ref[:, cols]) * pa + _sigmoid2(gb_ref[:, cols]) * pb
        o_ref[:, cols] = (0.5 * gated).astype(BF16)


def _rgm_prompt_kernel(xb_ref, zb_ref, ya_ref, ga_ref, gb_ref, cw_ref, cb_ref, wg_ref, br_ref,
                       bi_ref, lam_ref, wa_ref, wb_ref,
                       m_ref, hp_ref,
                       cv_ref, cn_ref, a_ref, cvb_ref, yb_cur_ref, yb_prev_ref, hx_ref, pv_ref, pr_ref, ps_ref):
    s = pl.program_id(0)
    groups = RGM_TM // SUB

    @pl.when(s == 0)
    def _init():
        for r in (hx_ref, pv_ref, pr_ref, ps_ref, yb_prev_ref, cv_ref, cvb_ref):
            r[...] = jnp.zeros_like(r)

    chunks = [slice(c * RG_LC, (c + 1) * RG_LC) for c in range(RG_WIDTH // RG_LC)]
    lo = lax.broadcasted_iota(jnp.int32, (SUB, RG_LC), 0) < SUB // 2

    _rg_gates(cv_ref, a_ref, wg_ref, br_ref, bi_ref, lam_ref, cvb_ref)
    _merge_tile(ya_ref, yb_prev_ref, ga_ref, gb_ref, wa_ref, wb_ref, m_ref)

    for lanes in chunks:
        (w0, w1, w2, w3), cb = _conv_taps(cw_ref, cb_ref, lanes)
        v_prev, r_prev, s_prev = pv_ref[:, lanes], pr_ref[:, lanes], ps_ref[:, lanes]
        for k in range(groups):
            rows = slice(k * SUB, (k + 1) * SUB)
            v = xb_ref[rows, lanes]
            r = _roll4(v)
            sh = jnp.where(lo, r_prev, r)
            cn_ref[rows, lanes] = cb + w3 * v + w2 * sh + w1 * v_prev + w0 * s_prev
            v_prev, r_prev, s_prev = v, r, sh
        pv_ref[:, lanes], pr_ref[:, lanes], ps_ref[:, lanes] = v_prev, r_prev, s_prev

    xs = [hx_ref[:, lanes] for lanes in chunks]
    for k in range(groups):
        rows = slice(k * SUB, (k + 1) * SUB)
        for c, lanes in enumerate(chunks):
            a = a_ref[rows, lanes]
            b = cv_ref[rows, lanes]
            y = a * _roll4(xs[c]) + b
            z = a * _roll4(y) + b
            xs[c] = jnp.where(lo, y, z)
            cv_ref[rows, lanes] = xs[c]
    for c, lanes in enumerate(chunks):
        hx_ref[:, lanes] = xs[c]

    rc = 64
    for j in range(RGM_TM // rc):
        rows = slice(j * rc, (j + 1) * rc)
        yb_cur_ref[rows, :] = (cv_ref[rows, :] * _silu(zb_ref[rows, :])).astype(BF16)
    yb_prev_ref[...] = yb_cur_ref[...]
    cv_ref[...] = cn_ref[...]
    cvb_ref[...] = cn_ref[...].astype(BF16)

    @pl.when(s == RGM_TILES)
    def _prompt_state():
        hp_ref[...] = hx_ref[...]


def _rgm_prompt(proj, ya, conv_w, conv_b, wg, b_r, b_i, lam, w_a, w_b, l):
    vec = _lspec(l, (1, RG_WIDTH))
    last = RGM_TILES - 1
    lag = lambda w, col, d: pl.BlockSpec((RGM_TM, w), lambda s: (jnp.clip(s - d, 0, last), col))
    once = dict(pipeline_mode=pl.Buffered(1))
    carry = pltpu.VMEM((SUB, RG_WIDTH), F32)
    return pl.pallas_call(
        _rgm_prompt_kernel,
        grid=(RGM_TILES + 2,),
        in_specs=[lag(RG_WIDTH, XB_COL, 0), lag(RG_WIDTH, ZB_COL, 1), lag(S5_WIDTH, 0, 2),
                  lag(D_MODEL, GA_COL, 2), lag(D_MODEL, GB_COL, 2),
                  _lspec(l, (CONV_W, RG_WIDTH)), vec,
                  _lspec(l, (RG_BLOCKS, RG_BLOCK, 2 * RG_BLOCK)), vec, vec, vec,
                  pl.BlockSpec((1, S5_WIDTH, D_MODEL), lambda s: (l, 0, 0), **once),
                  pl.BlockSpec((1, RG_WIDTH, D_MODEL), lambda s: (l, 0, 0), **once)],
        out_specs=[lag(D_MODEL, 0, 2), pl.BlockSpec((SUB, RG_WIDTH), lambda s: (0, 0))],
        out_shape=[jax.ShapeDtypeStruct((N_P, D_MODEL), BF16),
                   jax.ShapeDtypeStruct((SUB, RG_WIDTH), F32)],
        scratch_shapes=[pltpu.VMEM((RGM_TM, RG_WIDTH), F32),
                        pltpu.VMEM((RGM_TM, RG_WIDTH), F32),
                        pltpu.VMEM((RGM_TM, RG_WIDTH), F32),
                        pltpu.VMEM((RGM_TM, RG_WIDTH), BF16),
                        pltpu.VMEM((RGM_TM, RG_WIDTH), BF16),
                        pltpu.VMEM((RGM_TM, RG_WIDTH), BF16),
                        carry, carry, carry, carry],
        compiler_params=_cparams(1),
        name="rgm_prompt",
    )(proj, proj, ya, proj, proj, conv_w, conv_b, wg, b_r, b_i, lam, w_a, w_b)


def _rgm_sample_kernel(xb_ref, zb_ref, ya_ref, ga_ref, gb_ref, cw_ref, cb_ref, wg_ref,
                       br_ref, bi_ref, lam_ref, wa_ref, wb_ref, h0_ref, c0_ref,
                       m_ref, hs_ref,
                       cv_ref, a_ref, yb_ref):
    def src(t, r0, lanes):
        if t >= 0:
            return xb_ref[pl.ds(t * B_S + r0, SUB), lanes]
        return c0_ref[0, pl.ds((CONV_W - 1 + t) * B_S + r0, SUB), lanes]

    for c in range(RG_WIDTH // RG_LC):
        lanes = slice(c * RG_LC, (c + 1) * RG_LC)
        (w0, w1, w2, w3), cb = _conv_taps(cw_ref, cb_ref, lanes)

        def body(g, carry, w0=w0, w1=w1, w2=w2, w3=w3, cb=cb, lanes=lanes):
            r0 = pl.multiple_of(g * SUB, SUB)
            for t in range(T_S):
                cv_ref[pl.ds(t * B_S + r0, SUB), lanes] = (
                    cb + w3 * src(t, r0, lanes) + w2 * src(t - 1, r0, lanes)
                    + w1 * src(t - 2, r0, lanes) + w0 * src(t - 3, r0, lanes))
            return carry

        lax.fori_loop(0, B_S // SUB, body, 0)

    _rg_gates(cv_ref, a_ref, wg_ref, br_ref, bi_ref, lam_ref)

    for c in range(RG_WIDTH // RG_LC):
        lanes = slice(c * RG_LC, (c + 1) * RG_LC)

        def body(g, carry, lanes=lanes):
            r0 = pl.multiple_of(g * SUB, SUB)
            x = h0_ref[0, pl.ds(r0, SUB), lanes]
            for t in range(T_S):
                rows = pl.ds(t * B_S + r0, SUB)
                x = a_ref[rows, lanes] * x + cv_ref[rows, lanes]
                cv_ref[rows, lanes] = x
            hs_ref[pl.ds(r0, SUB), lanes] = x
            return carry

        lax.fori_loop(0, B_S // SUB, body, 0)

    rc = 64

    def out(j, carry):
        rows = pl.ds(pl.multiple_of(j * rc, rc), rc)
        yb_ref[rows, :] = (cv_ref[rows, :] * _silu(zb_ref[rows, :])).astype(BF16)
        return carry

    lax.fori_loop(0, N_S // rc, out, 0)
    _merge_tile(ya_ref, yb_ref, ga_ref, gb_ref, wa_ref, wb_ref, m_ref)


def _rgm_sample(proj, ya, conv_w, conv_b, wg, b_r, b_i, lam, w_a, w_b, h0, conv0, l):
    once = dict(pipeline_mode=pl.Buffered(1))
    tile = N_P // N_S
    rows = lambda w, col: pl.BlockSpec((N_S, w), lambda i: (tile, col), **once)
    lconst = lambda shape: pl.BlockSpec((1,) + shape, lambda i: (l,) + (0,) * len(shape), **once)
    vec = lconst((1, RG_WIDTH))
    return pl.pallas_call(
        _rgm_sample_kernel,
        grid=(1,),
        in_specs=[rows(RG_WIDTH, XB_COL), rows(RG_WIDTH, ZB_COL), rows(S5_WIDTH, 0),
                  rows(D_MODEL, GA_COL), rows(D_MODEL, GB_COL),
                  lconst((CONV_W, RG_WIDTH)), vec,
                  lconst((RG_BLOCKS, RG_BLOCK, 2 * RG_BLOCK)), vec, vec, vec,
                  lconst((S5_WIDTH, D_MODEL)), lconst((RG_WIDTH, D_MODEL)),
                  lconst((B_S, RG_WIDTH)), lconst(((CONV_W - 1) * B_S, RG_WIDTH))],
        out_specs=[pl.BlockSpec((N_S, D_MODEL), lambda i: (0, 0)),
                   pl.BlockSpec((B_S, RG_WIDTH), lambda i: (0, 0))],
        out_shape=[jax.ShapeDtypeStruct((N_S, D_MODEL), BF16),
                   jax.ShapeDtypeStruct((B_S, RG_WIDTH), F32)],
        scratch_shapes=[pltpu.VMEM((N_S, RG_WIDTH), F32),
                        pltpu.VMEM((N_S, RG_WIDTH), F32),
                        pltpu.VMEM((N_S, RG_WIDTH), BF16)],
        compiler_params=_cparams(1),
        name="rgm_sample",
    )(proj, proj, ya, proj, proj, conv_w, conv_b, wg, b_r, b_i, lam, w_a, w_b, h0, conv0)


OUT_TM = IO_TM


def _out_kernel(last, mp_ref, ms_ref, x_ref, w_ref, gate_ref, g_ref, *rest):
    if last:
        yp_ref, ys_ref = rest
    else:
        sc_ref, sh_ref, xo_ref, xn_ref = rest
    i = pl.program_id(0)
    tt = MOD_ROWS // B_P

    m = jnp.where(i < IO_TILES_P, mp_ref[...], ms_ref[...])
    out = _dot(m, w_ref[0])
    for s in range(OUT_TM // MOD_ROWS):
        rows = slice(s * MOD_ROWS, (s + 1) * MOD_ROWS)
        x = x_ref[rows, :] + gate_ref[0, 0, 0] * out[rows, :]
        y = _rms(x, g_ref[0])
        if last:
            @pl.when(i < IO_TILES_P)
            def _prompt(y=y, s=s):
                yp_ref[:, s * tt:(s + 1) * tt, :] = jnp.swapaxes(y.reshape(tt, B_P, D_MODEL), 0, 1)

            @pl.when(i >= IO_TILES_P)
            def _sample(y=y, rows=rows):
                ys_ref[rows, :] = y
        else:
            xo_ref[rows, :] = x
            xn_ref[rows, :] = (y * (1.0 + sc_ref[0, 0, 0]) + sh_ref[0, 0, 0]).astype(BF16)


def _out_layer(merged_p, merged_s, x, w_out, mods, l, gain, gain_l):
    last = l + 1 == DEPTH
    row = pl.BlockSpec((OUT_TM, D_MODEL), lambda i: (i, 0))
    row_p = pl.BlockSpec((OUT_TM, D_MODEL), lambda i: (jnp.minimum(i, IO_TILES_P - 1), 0))
    row_s = pl.BlockSpec((OUT_TM, D_MODEL), lambda i: (jnp.maximum(i - IO_TILES_P, 0), 0),
                         pipeline_mode=pl.Buffered(1))
    in_specs = [row_p, row_s, row,
                pl.BlockSpec((1, D_MODEL, D_MODEL), lambda i: (l, 0, 0),
                             pipeline_mode=pl.Buffered(1)),
                _mod_spec(l, MOD_GATE, OUT_TM),
                _lspec(gain_l, (1, D_MODEL))]
    args = [merged_p, merged_s, x, w_out, mods, gain]
    if last:
        out_specs = [_prompt_io_spec, _sample_io_spec]
        out_shape = [jax.ShapeDtypeStruct((B_P, T_P, D_MODEL), F32),
                     jax.ShapeDtypeStruct((N_S, D_MODEL), F32)]
    else:
        in_specs += [_mod_spec(l + 1, MOD_SCALE, OUT_TM), _mod_spec(l + 1, MOD_SHIFT, OUT_TM)]
        args += [mods, mods]
        out_specs = [row, row]
        out_shape = [jax.ShapeDtypeStruct((N_ROWS, D_MODEL), F32),
                     jax.ShapeDtypeStruct((N_ROWS, D_MODEL), BF16)]
    return pl.pallas_call(
        functools.partial(_out_kernel, last),
        grid=(N_ROWS // OUT_TM,),
        in_specs=in_specs,
        out_specs=out_specs,
        out_shape=out_shape,
        compiler_params=_cparams(1),
        name="out_last" if last else "out_layer",
    )(*args)


def _batch_major(rows, b, t):
    lead = rows.shape[:-2]
    return jnp.swapaxes(rows.reshape(lead + (t, b, rows.shape[-1])), -3, -2)


def _block_diag_in(bbt):
    gpc = S5_GROUPS // S5_CHUNKS
    w = bbt.reshape(DEPTH, S5_GROUP, S5_CHUNKS, gpc, S5_STATE)
    eye = jnp.eye(gpc, dtype=bbt.dtype)
    w = jnp.einsum("lkcgp,gh->lcgkhp", w, eye)
    return w.reshape(DEPTH, S5_CHUNKS, S5_CK, S5_CN)


def _block_diag_out(cmat):
    gpc = S5_GROUPS // S5_CHUNKS
    w = cmat.reshape(DEPTH, S5_CHUNKS, gpc, S5_GROUP, S5_STATE)
    eye = jnp.eye(gpc, dtype=cmat.dtype)
    w = jnp.einsum("lcgkp,gh->lchpgk", w, eye)
    return w.reshape(DEPTH, S5_CHUNKS, S5_CN, S5_CK)


def _mod_patterns(ada):
    a = ada.reshape(DEPTH, ada.shape[1], 3, D_MODEL).transpose(0, 2, 1, 3)
    p = jnp.tile(a[:, :, :B_P], (1, 1, MOD_ROWS // B_P, 1))
    s = a[:, :, B_P:B_P + B_S]
    return jnp.stack([p, s], axis=2)


def kernel(x_prompt, x_sample, state_s5_re, state_s5_im, state_rglru_h, state_conv,
           c_prompt, c_sample, w_ada, b_ada, norm_gain, w_in, b_in,
           s5_lam_re, s5_lam_im, s5_log_dt, s5_b_re, s5_b_im, s5_c_re, s5_c_im,
           s5_d, s5_w_glu, s5_b_glu, rg_conv_w, rg_conv_b, rg_w_r, rg_b_r, rg_w_i, rg_b_i,
           rg_lam, w_proj_a, w_proj_b, w_out, final_gain):
    sdt = state_s5_re.dtype

    to_lanes = lambda v: v.reshape(DEPTH, 1, S5_N)
    bt = lambda b: b.reshape(DEPTH, S5_N, S5_GROUP).transpose(0, 2, 1)
    log_dt = jnp.repeat(s5_log_dt, S5_STATE, axis=-1)
    a_re, a_im, bbt_re, bbt_im = _s5_prep(to_lanes(s5_lam_re), to_lanes(s5_lam_im),
                                          to_lanes(log_dt), bt(s5_b_re), bt(s5_b_im))
    wbu = jnp.concatenate([_block_diag_in(bbt_re), _block_diag_in(bbt_im)], axis=-2).astype(BF16)
    wc = jnp.concatenate([_block_diag_out(s5_c_re), _block_diag_out(s5_c_im)], axis=-1).astype(BF16)
    wg = jnp.concatenate([rg_w_r, rg_w_i], axis=-1).astype(BF16)
    wa_bf, wb_bf, wo_bf = w_proj_a.astype(BF16), w_proj_b.astype(BF16), w_out.astype(BF16)

    n_c = B_P + B_S
    c_all = jnp.concatenate([c_prompt, c_sample, jnp.zeros((-n_c % SUB, D_MODEL), F32)], axis=0)
    mods = _mod_patterns(_ada(c_all, w_ada, b_ada))

    h0_re = state_s5_re.reshape(DEPTH, B_S, S5_N)
    h0_im = state_s5_im.reshape(DEPTH, B_S, S5_N)
    conv0 = state_conv.transpose(0, 2, 1, 3).reshape(DEPTH, (CONV_W - 1) * B_S, RG_WIDTH)
    gains = jnp.concatenate([norm_gain, final_gain[None]], axis=0).reshape(DEPTH + 1, 1, D_MODEL)

    xs_tm = x_sample.transpose(1, 0, 2).reshape(N_S, D_MODEL)
    x, xn = _embed(x_prompt, xs_tm, gains, mods)

    outs = {k: [] for k in ("s5re_p", "s5im_p", "h_p", "conv_p", "s5re_s", "s5im_s", "h_s", "conv_s")}
    y_prompt = y_sample = None
    half = SUB // 2
    xb_cols = slice(2 * S5_WIDTH, 2 * S5_WIDTH + RG_WIDTH)
    tail = CONV_W - 1
    for l in range(DEPTH):
        proj = _proj(xn, w_in, _lvec(b_in), l)
        ya, hp_s5, hs_re, hs_im = _s5_branch(
            proj, wbu, wc, a_re, a_im, _lvec(s5_d), s5_w_glu, _lvec(s5_b_glu), h0_re, h0_im, l)
        rg_params = (rg_conv_w, _lvec(rg_conv_b), wg, _lvec(rg_b_r), _lvec(rg_b_i), _lvec(rg_lam),
                     wa_bf, wb_bf)
        merged_p, hp = _rgm_prompt(proj, ya, *rg_params, l)
        merged_s, hs = _rgm_sample(proj, ya, *rg_params, state_rglru_h, conv0, l)
        res = _out_layer(merged_p, merged_s, x, wo_bf, mods, l, gains, l + 1)
        if l + 1 < DEPTH:
            x, xn = res
        else:
            y_prompt, y_sample = res[0], _batch_major(res[1], B_S, T_S)

        outs["s5re_p"].append(hp_s5[:half])
        outs["s5im_p"].append(hp_s5[half:])
        outs["h_p"].append(hp[half:])
        outs["conv_p"].append(proj[N_P - tail * B_P:N_P, xb_cols])
        outs["s5re_s"].append(hs_re)
        outs["s5im_s"].append(hs_im)
        outs["h_s"].append(hs)
        outs["conv_s"].append(proj[N_ROWS - tail * B_S:, xb_cols])

    st = lambda k: jnp.stack(outs[k]).astype(sdt)
    s5_shape = lambda b: (DEPTH, b, S5_GROUPS, S5_STATE)
    return (y_prompt, y_sample,
            st("s5re_p").reshape(s5_shape(B_P)), st("s5im_p").reshape(s5_shape(B_P)),
            st("h_p"), _batch_major(st("conv_p"), B_P, tail),
            st("s5re_s").reshape(s5_shape(B_S)), st("s5im_s").reshape(s5_shape(B_S)),
            st("h_s"), _batch_major(st("conv_s"), B_S, tail))
```

```python
import functools

import jax
import jax.numpy as jnp
from jax import lax
from jax.experimental import pallas as pl
from jax.experimental.pallas import tpu as pltpu

F32 = jnp.float32
BF16 = jnp.bfloat16

D_MODEL = 2048
DEPTH = 4
B_P, T_P = 4, 2048
B_S, T_S = 128, 4
S5_WIDTH = 1024
S5_GROUP = 16
S5_GROUPS = 64
S5_STATE = 64
S5_N = S5_GROUPS * S5_STATE
RG_WIDTH = 2048
RG_BLOCKS = 16
RG_BLOCK = 128
RG_C = 8.0
CONV_W = 4
IN_COLS = 2 * S5_WIDTH + 2 * RG_WIDTH + 2 * D_MODEL
EPS = 1e-6
LOG2_E = 1.4426950408889634

TM = 512
N_P = B_P * T_P
N_S = B_S * T_S
N_ROWS = N_P + N_S
NP_TILES = N_P // TM
N_TILES = N_ROWS // TM
MOD_ROWS = 128
S5_CHUNKS = 8
S5_CK = S5_WIDTH // S5_CHUNKS
S5_CN = S5_N // S5_CHUNKS
SUB = 8
VMEM_LIMIT = 56 * 1024 * 1024

assert N_S == TM and N_P % TM == 0 and TM % MOD_ROWS == 0


def _cparams(n_axes, vmem=VMEM_LIMIT):
    return pltpu.CompilerParams(dimension_semantics=("arbitrary",) * n_axes,
                                vmem_limit_bytes=vmem)


def _dot(a, b):
    return jnp.dot(a, b, preferred_element_type=F32)


def _roll4(x):
    return pltpu.roll(x, SUB // 2, 0)


def _sigmoid(x):
    return 0.5 * jnp.tanh(0.5 * x) + 0.5


def _sigmoid2(x):
    return jnp.tanh(0.5 * x) + 1.0


def _lvec(a):
    return a.reshape(DEPTH, 1, a.shape[-1])


def _lspec(l, shape, n_axes=1):
    zeros = (0,) * len(shape)
    if n_axes == 1:
        return pl.BlockSpec((1,) + tuple(shape), lambda i: (l,) + zeros)
    return pl.BlockSpec((1,) + tuple(shape), lambda j, i: (l,) + zeros)


def _s5_prep_kernel(lre_ref, lim_ref, ldt_ref, bre_ref, bim_ref,
                    ar_ref, ai_ref, bbre_ref, bbim_ref):
    lam_re = lre_ref[0]
    lam_im = lim_ref[0]
    dt = jnp.exp(ldt_ref[0])
    mag = jnp.exp(lam_re * dt)
    a_re = mag * jnp.cos(lam_im * dt)
    a_im = mag * jnp.sin(lam_im * dt)
    nr = a_re - 1.0
    ni = a_im
    den = lam_re * lam_re + lam_im * lam_im
    c_re = (nr * lam_re + ni * lam_im) / den
    c_im = (ni * lam_re - nr * lam_im) / den
    b_re = bre_ref[0]
    b_im = bim_ref[0]
    bbre_ref[0] = c_re * b_re - c_im * b_im
    bbim_ref[0] = c_re * b_im + c_im * b_re
    ar_ref[0] = a_re
    ai_ref[0] = a_im


def _s5_prep(lam_re, lam_im, log_dt, bt_re, bt_im):
    vec = pl.BlockSpec((1, 1, S5_N), lambda l: (l, 0, 0))
    mat = pl.BlockSpec((1, S5_GROUP, S5_N), lambda l: (l, 0, 0))
    return pl.pallas_call(
        _s5_prep_kernel,
        grid=(DEPTH,),
        in_specs=[vec, vec, vec, mat, mat],
        out_specs=[vec, vec, mat, mat],
        out_shape=[jax.ShapeDtypeStruct((DEPTH, 1, S5_N), F32)] * 2
        + [jax.ShapeDtypeStruct((DEPTH, S5_GROUP, S5_N), F32)] * 2,
        compiler_params=_cparams(1),
        name="s5_prep",
    )(lam_re, lam_im, log_dt, bt_re, bt_im)


ADA_TN = 1024


def _ada_kernel(c_ref, w_ref, b_ref, o_ref):
    o_ref[0] = _dot(c_ref[...].astype(BF16), w_ref[0].astype(BF16)) + b_ref[0]


def _ada(c_all, w_ada, b_ada):
    rows = c_all.shape[0]
    return pl.pallas_call(
        _ada_kernel,
        grid=(DEPTH, 3 * D_MODEL // ADA_TN),
        in_specs=[pl.BlockSpec((rows, D_MODEL), lambda l, j: (0, 0)),
                  pl.BlockSpec((1, D_MODEL, ADA_TN), lambda l, j: (l, 0, j)),
                  pl.BlockSpec((1, 1, ADA_TN), lambda l, j: (l, 0, j))],
        out_specs=pl.BlockSpec((1, rows, ADA_TN), lambda l, j: (l, 0, j)),
        out_shape=jax.ShapeDtypeStruct((DEPTH, rows, 3 * D_MODEL), F32),
        compiler_params=_cparams(2),
        name="ada",
    )(c_all, w_ada, _lvec(b_ada))


MOD_SHIFT, MOD_SCALE, MOD_GATE = 0, 1, 2


def _mod_spec(l, kind, tm):
    tiles_p = N_P // tm
    return pl.BlockSpec((1, 1, 1, MOD_ROWS, D_MODEL), lambda i: (l, kind, i // tiles_p, 0, 0))


def _rms(x, gain):
    ms = jnp.mean(x * x, axis=-1, keepdims=True)
    return x * lax.rsqrt(ms + EPS) * gain


IO_TM = TM
IO_TT = IO_TM // B_P
IO_TILES_P = N_P // IO_TM

_prompt_io_spec = pl.BlockSpec((B_P, IO_TT, D_MODEL), lambda i: (0, jnp.minimum(i, IO_TILES_P - 1), 0))
_sample_io_spec = pl.BlockSpec((IO_TM, D_MODEL), lambda i: (jnp.maximum(i - IO_TILES_P, 0), 0))


def _embed_kernel(xp_ref, xs_ref, g_ref, sc_ref, sh_ref, xo_ref, xn_ref):
    i = pl.program_id(0)

    @pl.when(i < IO_TILES_P)
    def _prompt():
        xo_ref[...] = jnp.swapaxes(xp_ref[...], 0, 1).reshape(IO_TM, D_MODEL)

    @pl.when(i >= IO_TILES_P)
    def _sample():
        xo_ref[...] = xs_ref[...]

    for s in range(IO_TM // MOD_ROWS):
        rows = slice(s * MOD_ROWS, (s + 1) * MOD_ROWS)
        y = _rms(xo_ref[rows, :], g_ref[0])
        xn_ref[rows, :] = (y * (1.0 + sc_ref[0, 0, 0]) + sh_ref[0, 0, 0]).astype(BF16)


def _embed(x_prompt, x_sample, gain, mods):
    row = pl.BlockSpec((IO_TM, D_MODEL), lambda i: (i, 0))
    return pl.pallas_call(
        _embed_kernel,
        grid=(N_ROWS // IO_TM,),
        in_specs=[_prompt_io_spec, _sample_io_spec, _lspec(0, (1, D_MODEL)),
                  _mod_spec(0, MOD_SCALE, IO_TM), _mod_spec(0, MOD_SHIFT, IO_TM)],
        out_specs=[row, row],
        out_shape=[jax.ShapeDtypeStruct((N_ROWS, D_MODEL), F32),
                   jax.ShapeDtypeStruct((N_ROWS, D_MODEL), BF16)],
        compiler_params=_cparams(1),
        name="embed",
    )(x_prompt, x_sample, gain, mods, mods)


PROJ_TN = 2048
PROJ_TM = N_ROWS // 16
assert N_ROWS % PROJ_TM == 0 and PROJ_TM % (2 * SUB) == 0


def _silu(z):
    hz = 0.5 * z
    return hz * (jnp.tanh(hz) + 1.0)


def _proj_kernel(xn_ref, w_ref, b_ref, o_ref, wbf_ref):
    @pl.when(pl.program_id(1) == 0)
    def _():
        wbf_ref[...] = w_ref[0].astype(BF16)

    o_ref[...] = _dot(xn_ref[...], wbf_ref[...]) + b_ref[0]


def _proj(xn, w_in, b_in, l):
    return pl.pallas_call(
        _proj_kernel,
        grid=(IN_COLS // PROJ_TN, N_ROWS // PROJ_TM),
        in_specs=[pl.BlockSpec((PROJ_TM, D_MODEL), lambda j, i: (i, 0)),
                  pl.BlockSpec((1, D_MODEL, PROJ_TN), lambda j, i: (l, 0, j)),
                  pl.BlockSpec((1, 1, PROJ_TN), lambda j, i: (l, 0, j))],
        out_specs=pl.BlockSpec((PROJ_TM, PROJ_TN), lambda j, i: (i, j)),
        out_shape=jax.ShapeDtypeStruct((N_ROWS, IN_COLS), F32),
        scratch_shapes=[pltpu.VMEM((D_MODEL, PROJ_TN), BF16)],
        compiler_params=_cparams(2),
        name="proj_in",
    )(xn, w_in, b_in)


def _gelu(y):
    return 0.5 * y * (1.0 + lax.erf(y * (2.0 ** -0.5)))


def _s5_kernel(u_ref, z_ref, wbu_ref, wc_ref, ar_ref, ai_ref, d_ref,
               wglu_ref, bglu_ref, h0re_ref, h0im_ref,
               ya_ref, hp_ref, hsre_ref, hsim_ref,
               bu_ref, hs_ref, y_ref, yact_ref, g_ref, p_ref, wglu_bf_ref):
    i = pl.program_id(0)
    half = SUB // 2

    @pl.when(i == 0)
    def _init():
        wglu_bf_ref[...] = wglu_ref[0].astype(BF16)
        p_ref[...] = jnp.zeros_like(p_ref)

    def abar(c):
        lanes = slice(c * S5_CN, (c + 1) * S5_CN)
        a_re = jnp.broadcast_to(ar_ref[0, :, lanes], (SUB, S5_CN))
        a_im = jnp.broadcast_to(ai_ref[0, :, lanes], (SUB, S5_CN))
        return a_re, a_im

    @pl.when(i < NP_TILES)
    def _prompt():
        groups = TM // SUB
        lo3 = lax.broadcasted_iota(jnp.int32, (groups, SUB, S5_CK), 1) < half
        lo = lax.broadcasted_iota(jnp.int32, (SUB, S5_CN), 0) < half
        for c in range(S5_CHUNKS):
            lanes = slice(c * S5_CN, (c + 1) * S5_CN)
            chans = slice(c * S5_CK, (c + 1) * S5_CK)
            buf = bu_ref.at[c % 2]
            hs = hs_ref.at[c % 2]
            v = u_ref[:, chans].reshape(groups, SUB, S5_CK)
            r = pltpu.roll(v, half, 1)
            zero = jnp.zeros_like(v)
            t_even = jnp.concatenate([jnp.where(lo3, v, zero), jnp.where(lo3, zero, r)], axis=-1)
            t_odd = jnp.concatenate([jnp.where(lo3, r, zero), jnp.where(lo3, zero, v)], axis=-1)
            u2 = jnp.stack([t_even, t_odd], axis=1).reshape(2 * TM, 2 * S5_CK)
            buf[...] = _dot(u2.astype(BF16), wbu_ref[0, c])

            a_re, a_im = abar(c)
            a_sw = jnp.where(lo, -a_im, a_im)
            p = p_ref[:, lanes]
            for t in range(TM // B_P):
                rows = slice(t * SUB, (t + 1) * SUB)
                p = a_re * p + a_sw * _roll4(p) + buf[rows, :]
                hs[rows, :] = p
            p_ref[:, lanes] = p

            q = _dot(hs[...].astype(BF16), wc_ref[0, c])
            q = q.reshape(2 * groups, SUB, 2 * S5_CK)
            dd = q[:, :, :S5_CK] - pltpu.roll(q[:, :, S5_CK:], half, 1)
            dd = dd.reshape(groups, 2, SUB, S5_CK)
            yv = jnp.where(lo3, dd[:, 0], pltpu.roll(dd[:, 1], half, 1))
            y_ref[:, chans] = yv.reshape(TM, S5_CK)

    @pl.when(i == NP_TILES)
    def _sample():
        for c in range(S5_CHUNKS):
            lanes = slice(c * S5_CN, (c + 1) * S5_CN)
            chans = slice(c * S5_CK, (c + 1) * S5_CK)
            buf = bu_ref.at[c % 2]
            ub = u_ref[:, chans].astype(BF16)
            buf[0:TM, :] = _dot(ub, wbu_ref[0, c, 0:S5_CK, :])
            buf[TM:2 * TM, :] = _dot(ub, wbu_ref[0, c, S5_CK:2 * S5_CK, :])
            a_re, a_im = abar(c)

            for r0 in range(0, B_S, SUB):
                xr = h0re_ref[0, r0:r0 + SUB, lanes]
                xi = h0im_ref[0, r0:r0 + SUB, lanes]
                for t in range(T_S):
                    rre = slice(t * B_S + r0, t * B_S + r0 + SUB)
                    rim = slice(TM + t * B_S + r0, TM + t * B_S + r0 + SUB)
                    xr, xi = (a_re * xr - a_im * xi + buf[rre, :], a_re * xi + a_im * xr + buf[rim, :])
                    buf[rre, :] = xr
                    buf[rim, :] = xi
                hsre_ref[r0:r0 + SUB, lanes] = xr
                hsim_ref[r0:r0 + SUB, lanes] = xi
            y_ref[:, chans] = (_dot(buf[0:TM, :].astype(BF16), wc_ref[0, c, :, 0:S5_CK])
                               - _dot(buf[TM:2 * TM, :].astype(BF16), wc_ref[0, c, :, S5_CK:2 * S5_CK]))

    @pl.when(i == NP_TILES - 1)
    def _prompt_state():
        hp_ref[...] = p_ref[...]

    rc = 64
    halves = [range(h * (TM // 2), (h + 1) * (TM // 2), rc) for h in range(2)]

    def act(r0):
        rows = slice(r0, r0 + rc)
        a = _gelu(y_ref[rows, :] + d_ref[0] * u_ref[rows, :])
        y_ref[rows, :] = a
        yact_ref[rows, :] = a.astype(BF16)

    def glu(h):
        rows = slice(halves[h][0], halves[h][0] + TM // 2)
        g_ref[rows, :] = _dot(yact_ref[rows, :], wglu_bf_ref[...])

    def gate(r0):
        rows = slice(r0, r0 + rc)
        y = y_ref[rows, :] * _sigmoid(g_ref[rows, :] + bglu_ref[0])
        ya_ref[rows, :] = (y * _silu(z_ref[rows, :])).astype(BF16)

    for r0 in halves[0]:
        act(r0)
    glu(0)
    for r0 in halves[1]:
        act(r0)
    glu(1)
    for r0 in halves[0]:
        gate(r0)
    for r0 in halves[1]:
        gate(r0)


def _s5_branch(proj, wbu, wc, a_re, a_im, d, w_glu, b_glu, h0_re, h0_im, l):
    const2 = lambda i: (0, 0)
    state_s = pl.BlockSpec((B_S, S5_N), const2)
    once = dict(pipeline_mode=pl.Buffered(1))
    lconst = lambda shape: pl.BlockSpec((1,) + shape, lambda i: (l,) + (0,) * len(shape), **once)
    return pl.pallas_call(
        _s5_kernel,
        grid=(N_TILES,),
        in_specs=[pl.BlockSpec((TM, S5_WIDTH), lambda i: (i, 0)),
                  pl.BlockSpec((TM, S5_WIDTH), lambda i: (i, 1)),
                  lconst((S5_CHUNKS, 2 * S5_CK, S5_CN)),
                  lconst((S5_CHUNKS, S5_CN, 2 * S5_CK)),
                  _lspec(l, (1, S5_N)),
                  _lspec(l, (1, S5_N)),
                  _lspec(l, (1, S5_WIDTH)),
                  lconst((S5_WIDTH, S5_WIDTH)),
                  _lspec(l, (1, S5_WIDTH)),
                  lconst((B_S, S5_N)),
                  lconst((B_S, S5_N))],
        out_specs=[pl.BlockSpec((TM, S5_WIDTH), lambda i: (i, 0)),
                   pl.BlockSpec((SUB, S5_N), const2), state_s, state_s],
        out_shape=[jax.ShapeDtypeStruct((N_ROWS, S5_WIDTH), BF16),
                   jax.ShapeDtypeStruct((SUB, S5_N), F32),
                   jax.ShapeDtypeStruct((B_S, S5_N), F32),
                   jax.ShapeDtypeStruct((B_S, S5_N), F32)],
        scratch_shapes=[pltpu.VMEM((2, 2 * TM, S5_CN), F32),
                        pltpu.VMEM((2, 2 * TM, S5_CN), F32),
                        pltpu.VMEM((TM, S5_WIDTH), F32),
                        pltpu.VMEM((TM, S5_WIDTH), BF16),
                        pltpu.VMEM((TM, S5_WIDTH), F32),
                        pltpu.VMEM((SUB, S5_N), F32),
                        pltpu.VMEM((S5_WIDTH, S5_WIDTH), BF16)],
        compiler_params=_cparams(1),
        name="s5_branch",
    )(proj, proj, wbu, wc, a_re, a_im, d, w_glu, b_glu, h0_re, h0_im)


RG_LC = 512


RGM_TM = 256
RGM_TILES = N_P // RGM_TM
MERGE_TN = 1024
XB_COL, ZB_COL, GA_COL, GB_COL = 1, 2, 3, 4
assert RG_WIDTH == D_MODEL and 2 * S5_WIDTH == RG_WIDTH


def _conv_taps(cw_ref, cb_ref, lanes):
    taps = [jnp.broadcast_to(cw_ref[0, k:k + 1, lanes], (SUB, RG_LC)) for k in range(CONV_W)]
    return taps, jnp.broadcast_to(cb_ref[0, :, lanes], (SUB, RG_LC))


def _rg_gates(cv_ref, a_ref, wg_ref, br_ref, bi_ref, lam_ref, cvb_ref=None):
    for h in range(RG_BLOCKS):
        lanes = slice(h * RG_BLOCK, (h + 1) * RG_BLOCK)
        sp = lam_ref[0, :, lanes]
        c_lam = -RG_C * (jnp.maximum(-sp, 0.0) + jnp.log1p(jnp.exp(-jnp.abs(sp))))
        k = c_lam * (0.5 * LOG2_E)
        cv = cv_ref[:, lanes]
        cvb = cv.astype(BF16) if cvb_ref is None else cvb_ref[:, lanes]
        g = _dot(cvb, wg_ref[0, h])
        a = jnp.exp2(k * jnp.tanh(0.5 * (g[:, :RG_BLOCK] + br_ref[0, :, lanes])) + k)
        gi = _sigmoid(g[:, RG_BLOCK:] + bi_ref[0, :, lanes])
        m2 = 1.0 - a * a
        mult = jnp.where(m2 > 0.0, m2 * lax.rsqrt(m2), 0.0)
        a_ref[:, lanes] = a
        cv_ref[:, lanes] = mult * gi * cv


def _merge_tile(ya_ref, yb_ref, ga_ref, gb_ref, wa_ref, wb_ref, o_ref, chunks=range(D_MODEL // MERGE_TN)):
    for j in chunks:
        cols = slice(j * MERGE_TN, (j + 1) * MERGE_TN)
        pa = _dot(ya_ref[...], wa_ref[0, :, cols])
        pb = _dot(yb_ref[...], wb_ref[0, :, cols])
        gated = _sigmoid2(ga_ref[:, cols]) * pa + _sigmoid2(gb_ref[:, cols]) * pb
        o_ref[:, cols] = (0.5 * gated).astype(BF16)


def _rgm_prompt_kernel(xb_ref, zb_ref, ya_ref, ga_ref, gb_ref, cw_ref, cb_ref, wg_ref, br_ref,
                       bi_ref, lam_ref, wa_ref, wb_ref,
                       m_ref, hp_ref,
                       cv_ref, cn_ref, a_ref, hv_ref, cvb_ref, yb_cur_ref, yb_prev_ref,
                       hx_ref, pv_ref, pr_ref, ps_ref):
    s = pl.program_id(0)
    groups = RGM_TM // SUB

    @pl.when(s == 0)
    def _init():
        for r in (hx_ref, pv_ref, pr_ref, ps_ref, yb_prev_ref, cv_ref, cvb_ref):
            r[...] = jnp.zeros_like(r)

    chunks = [slice(c * RG_LC, (c + 1) * RG_LC) for c in range(RG_WIDTH // RG_LC)]
    lo = lax.broadcasted_iota(jnp.int32, (SUB, RG_LC), 0) < SUB // 2

    _rg_gates(cv_ref, a_ref, wg_ref, br_ref, bi_ref, lam_ref, cvb_ref)
    _merge_tile(ya_ref, yb_prev_ref, ga_ref, gb_ref, wa_ref, wb_ref, m_ref)

    for lanes in chunks:
        (w0, w1, w2, w3), cb = _conv_taps(cw_ref, cb_ref, lanes)
        v_prev, r_prev, s_prev = pv_ref[:, lanes], pr_ref[:, lanes], ps_ref[:, lanes]
        for k in range(groups):
            rows = slice(k * SUB, (k + 1) * SUB)
            v = xb_ref[rows, lanes]
            r = _roll4(v)
            sh = jnp.where(lo, r_prev, r)
            cn_ref[rows, lanes] = cb + w3 * v + w2 * sh + w1 * v_prev + w0 * s_prev
            v_prev, r_prev, s_prev = v, r, sh
        pv_ref[:, lanes], pr_ref[:, lanes], ps_ref[:, lanes] = v_prev, r_prev, s_prev

    xs = [hx_ref[:, lanes] for lanes in chunks]
    for k in range(groups):
        rows = slice(k * SUB, (k + 1) * SUB)
        for c, lanes in enumerate(chunks):
            a = a_ref[rows, lanes]
            b = cv_ref[rows, lanes]
            y = a * _roll4(xs[c]) + b
            z = a * _roll4(y) + b
            xs[c] = jnp.where(lo, y, z)
            hv_ref[rows, lanes] = xs[c]
    for c, lanes in enumerate(chunks):
        hx_ref[:, lanes] = xs[c]

    rc = 64
    for j in range(RGM_TM // rc):
        rows = slice(j * rc, (j + 1) * rc)
        yb_cur_ref[rows, :] = (hv_ref[rows, :] * _silu(zb_ref[rows, :])).astype(BF16)
    yb_prev_ref[...] = yb_cur_ref[...]
    cv_ref[...] = cn_ref[...]
    cvb_ref[...] = cn_ref[...].astype(BF16)

    @pl.when(s == RGM_TILES)
    def _prompt_state():
        hp_ref[...] = hx_ref[...]


def _rgm_prompt(proj, ya, conv_w, conv_b, wg, b_r, b_i, lam, w_a, w_b, l):
    vec = _lspec(l, (1, RG_WIDTH))
    last = RGM_TILES - 1
    lag = lambda w, col, d: pl.BlockSpec((RGM_TM, w), lambda s: (jnp.clip(s - d, 0, last), col))
    once = dict(pipeline_mode=pl.Buffered(1))
    carry = pltpu.VMEM((SUB, RG_WIDTH), F32)
    return pl.pallas_call(
        _rgm_prompt_kernel,
        grid=(RGM_TILES + 2,),
        in_specs=[lag(RG_WIDTH, XB_COL, 0), lag(RG_WIDTH, ZB_COL, 1), lag(S5_WIDTH, 0, 2),
                  lag(D_MODEL, GA_COL, 2), lag(D_MODEL, GB_COL, 2),
                  _lspec(l, (CONV_W, RG_WIDTH)), vec,
                  _lspec(l, (RG_BLOCKS, RG_BLOCK, 2 * RG_BLOCK)), vec, vec, vec,
                  pl.BlockSpec((1, S5_WIDTH, D_MODEL), lambda s: (l, 0, 0), **once),
                  pl.BlockSpec((1, RG_WIDTH, D_MODEL), lambda s: (l, 0, 0), **once)],
        out_specs=[lag(D_MODEL, 0, 2), pl.BlockSpec((SUB, RG_WIDTH), lambda s: (0, 0))],
        out_shape=[jax.ShapeDtypeStruct((N_P, D_MODEL), BF16),
                   jax.ShapeDtypeStruct((SUB, RG_WIDTH), F32)],
        scratch_shapes=[pltpu.VMEM((RGM_TM, RG_WIDTH), F32),
                        pltpu.VMEM((RGM_TM, RG_WIDTH), F32),
                        pltpu.VMEM((RGM_TM, RG_WIDTH), F32),
                        pltpu.VMEM((RGM_TM, RG_WIDTH), F32),
                        pltpu.VMEM((RGM_TM, RG_WIDTH), BF16),
                        pltpu.VMEM((RGM_TM, RG_WIDTH), BF16),
                        pltpu.VMEM((RGM_TM, RG_WIDTH), BF16),
                        carry, carry, carry, carry],
        compiler_params=_cparams(1),
        name="rgm_prompt",
    )(proj, proj, ya, proj, proj, conv_w, conv_b, wg, b_r, b_i, lam, w_a, w_b)


def _rgm_sample_kernel(xb_ref, zb_ref, ya_ref, ga_ref, gb_ref, cw_ref, cb_ref, wg_ref,
                       br_ref, bi_ref, lam_ref, wa_ref, wb_ref, h0_ref, c0_ref,
                       m_ref, hs_ref,
                       cv_ref, a_ref, yb_ref):
    def src(t, r0, lanes):
        if t >= 0:
            return xb_ref[pl.ds(t * B_S + r0, SUB), lanes]
        return c0_ref[0, pl.ds((CONV_W - 1 + t) * B_S + r0, SUB), lanes]

    for c in range(RG_WIDTH // RG_LC):
        lanes = slice(c * RG_LC, (c + 1) * RG_LC)
        (w0, w1, w2, w3), cb = _conv_taps(cw_ref, cb_ref, lanes)

        def body(g, carry, w0=w0, w1=w1, w2=w2, w3=w3, cb=cb, lanes=lanes):
            r0 = pl.multiple_of(g * SUB, SUB)
            for t in range(T_S):
                cv_ref[pl.ds(t * B_S + r0, SUB), lanes] = (
                    cb + w3 * src(t, r0, lanes) + w2 * src(t - 1, r0, lanes)
                    + w1 * src(t - 2, r0, lanes) + w0 * src(t - 3, r0, lanes))
            return carry

        lax.fori_loop(0, B_S // SUB, body, 0)

    _rg_gates(cv_ref, a_ref, wg_ref, br_ref, bi_ref, lam_ref)

    for c in range(RG_WIDTH // RG_LC):
        lanes = slice(c * RG_LC, (c + 1) * RG_LC)

        def body(g, carry, lanes=lanes):
            r0 = pl.multiple_of(g * SUB, SUB)
            x = h0_ref[0, pl.ds(r0, SUB), lanes]
            for t in range(T_S):
                rows = pl.ds(t * B_S + r0, SUB)
                x = a_ref[rows, lanes] * x + cv_ref[rows, lanes]
                cv_ref[rows, lanes] = x
            hs_ref[pl.ds(r0, SUB), lanes] = x
            return carry

        lax.fori_loop(0, B_S // SUB, body, 0)

    rc = 64

    def out(j, carry):
        rows = pl.ds(pl.multiple_of(j * rc, rc), rc)
        yb_ref[rows, :] = (cv_ref[rows, :] * _silu(zb_ref[rows, :])).astype(BF16)
        return carry

    lax.fori_loop(0, N_S // rc, out, 0)
    _merge_tile(ya_ref, yb_ref, ga_ref, gb_ref, wa_ref, wb_ref, m_ref)


def _rgm_sample(proj, ya, conv_w, conv_b, wg, b_r, b_i, lam, w_a, w_b, h0, conv0, l):
    once = dict(pipeline_mode=pl.Buffered(1))
    tile = N_P // N_S
    rows = lambda w, col: pl.BlockSpec((N_S, w), lambda i: (tile, col), **once)
    lconst = lambda shape: pl.BlockSpec((1,) + shape, lambda i: (l,) + (0,) * len(shape), **once)
    vec = lconst((1, RG_WIDTH))
    return pl.pallas_call(
        _rgm_sample_kernel,
        grid=(1,),
        in_specs=[rows(RG_WIDTH, XB_COL), rows(RG_WIDTH, ZB_COL), rows(S5_WIDTH, 0),
                  rows(D_MODEL, GA_COL), rows(D_MODEL, GB_COL),
                  lconst((CONV_W, RG_WIDTH)), vec,
                  lconst((RG_BLOCKS, RG_BLOCK, 2 * RG_BLOCK)), vec, vec, vec,
                  lconst((S5_WIDTH, D_MODEL)), lconst((RG_WIDTH, D_MODEL)),
                  lconst((B_S, RG_WIDTH)), lconst(((CONV_W - 1) * B_S, RG_WIDTH))],
        out_specs=[pl.BlockSpec((N_S, D_MODEL), lambda i: (0, 0)),
                   pl.BlockSpec((B_S, RG_WIDTH), lambda i: (0, 0))],
        out_shape=[jax.ShapeDtypeStruct((N_S, D_MODEL), BF16),
                   jax.ShapeDtypeStruct((B_S, RG_WIDTH), F32)],
        scratch_shapes=[pltpu.VMEM((N_S, RG_WIDTH), F32),
                        pltpu.VMEM((N_S, RG_WIDTH), F32),
                        pltpu.VMEM((N_S, RG_WIDTH), BF16)],
        compiler_params=_cparams(1),
        name="rgm_sample",
    )(proj, proj, ya, proj, proj, conv_w, conv_b, wg, b_r, b_i, lam, w_a, w_b, h0, conv0)


OUT_TM = IO_TM


def _out_kernel(last, mp_ref, ms_ref, x_ref, w_ref, gate_ref, g_ref, *rest):
    if last:
        yp_ref, ys_ref = rest
    else:
        sc_ref, sh_ref, xo_ref, xn_ref = rest
    i = pl.program_id(0)
    tt = MOD_ROWS // B_P

    m = jnp.where(i < IO_TILES_P, mp_ref[...], ms_ref[...])
    out = _dot(m, w_ref[0])
    for s in range(OUT_TM // MOD_ROWS):
        rows = slice(s * MOD_ROWS, (s + 1) * MOD_ROWS)
        x = x_ref[rows, :] + gate_ref[0, 0, 0] * out[rows, :]
        y = _rms(x, g_ref[0])
        if last:
            @pl.when(i < IO_TILES_P)
            def _prompt(y=y, s=s):
                yp_ref[:, s * tt:(s + 1) * tt, :] = jnp.swapaxes(y.reshape(tt, B_P, D_MODEL), 0, 1)

            @pl.when(i >= IO_TILES_P)
            def _sample(y=y, rows=rows):
                ys_ref[rows, :] = y
        else:
            xo_ref[rows, :] = x
            xn_ref[rows, :] = (y * (1.0 + sc_ref[0, 0, 0]) + sh_ref[0, 0, 0]).astype(BF16)


def _out_layer(merged_p, merged_s, x, w_out, mods, l, gain, gain_l):
    last = l + 1 == DEPTH
    row = pl.BlockSpec((OUT_TM, D_MODEL), lambda i: (i, 0))
    row_p = pl.BlockSpec((OUT_TM, D_MODEL), lambda i: (jnp.minimum(i, IO_TILES_P - 1), 0))
    row_s = pl.BlockSpec((OUT_TM, D_MODEL), lambda i: (jnp.maximum(i - IO_TILES_P, 0), 0),
                         pipeline_mode=pl.Buffered(1))
    in_specs = [row_p, row_s, row,
                pl.BlockSpec((1, D_MODEL, D_MODEL), lambda i: (l, 0, 0),
                             pipeline_mode=pl.Buffered(1)),
                _mod_spec(l, MOD_GATE, OUT_TM),
                _lspec(gain_l, (1, D_MODEL))]
    args = [merged_p, merged_s, x, w_out, mods, gain]
    if last:
        out_specs = [_prompt_io_spec, _sample_io_spec]
        out_shape = [jax.ShapeDtypeStruct((B_P, T_P, D_MODEL), F32),
                     jax.ShapeDtypeStruct((N_S, D_MODEL), F32)]
    else:
        in_specs += [_mod_spec(l + 1, MOD_SCALE, OUT_TM), _mod_spec(l + 1, MOD_SHIFT, OUT_TM)]
        args += [mods, mods]
        out_specs = [row, row]
        out_shape = [jax.ShapeDtypeStruct((N_ROWS, D_MODEL), F32),
                     jax.ShapeDtypeStruct((N_ROWS, D_MODEL), BF16)]
    return pl.pallas_call(
        functools.partial(_out_kernel, last),
        grid=(N_ROWS // OUT_TM,),
        in_specs=in_specs,
        out_specs=out_specs,
        out_shape=out_shape,
        compiler_params=_cparams(1),
        name="out_last" if last else "out_layer",
    )(*args)


def _batch_major(rows, b, t):
    lead = rows.shape[:-2]
    return jnp.swapaxes(rows.reshape(lead + (t, b, rows.shape[-1])), -3, -2)


def _block_diag_in(bbt):
    gpc = S5_GROUPS // S5_CHUNKS
    w = bbt.reshape(DEPTH, S5_GROUP, S5_CHUNKS, gpc, S5_STATE)
    eye = jnp.eye(gpc, dtype=bbt.dtype)
    w = jnp.einsum("lkcgp,gh->lcgkhp", w, eye)
    return w.reshape(DEPTH, S5_CHUNKS, S5_CK, S5_CN)


def _block_diag_out(cmat):
    gpc = S5_GROUPS // S5_CHUNKS
    w = cmat.reshape(DEPTH, S5_CHUNKS, gpc, S5_GROUP, S5_STATE)
    eye = jnp.eye(gpc, dtype=cmat.dtype)
    w = jnp.einsum("lcgkp,gh->lchpgk", w, eye)
    return w.reshape(DEPTH, S5_CHUNKS, S5_CN, S5_CK)


def _mod_patterns(ada):
    a = ada.reshape(DEPTH, ada.shape[1], 3, D_MODEL).transpose(0, 2, 1, 3)
    p = jnp.tile(a[:, :, :B_P], (1, 1, MOD_ROWS // B_P, 1))
    s = a[:, :, B_P:B_P + B_S]
    return jnp.stack([p, s], axis=2)


def kernel(x_prompt, x_sample, state_s5_re, state_s5_im, state_rglru_h, state_conv,
           c_prompt, c_sample, w_ada, b_ada, norm_gain, w_in, b_in,
           s5_lam_re, s5_lam_im, s5_log_dt, s5_b_re, s5_b_im, s5_c_re, s5_c_im,
           s5_d, s5_w_glu, s5_b_glu, rg_conv_w, rg_conv_b, rg_w_r, rg_b_r, rg_w_i, rg_b_i,
           rg_lam, w_proj_a, w_proj_b, w_out, final_gain):
    sdt = state_s5_re.dtype

    to_lanes = lambda v: v.reshape(DEPTH, 1, S5_N)
    bt = lambda b: b.reshape(DEPTH, S5_N, S5_GROUP).transpose(0, 2, 1)
    log_dt = jnp.repeat(s5_log_dt, S5_STATE, axis=-1)
    a_re, a_im, bbt_re, bbt_im = _s5_prep(to_lanes(s5_lam_re), to_lanes(s5_lam_im),
                                          to_lanes(log_dt), bt(s5_b_re), bt(s5_b_im))
    wbu = jnp.concatenate([_block_diag_in(bbt_re), _block_diag_in(bbt_im)], axis=-2).astype(BF16)
    wc = jnp.concatenate([_block_diag_out(s5_c_re), _block_diag_out(s5_c_im)], axis=-1).astype(BF16)
    wg = jnp.concatenate([rg_w_r, rg_w_i], axis=-1).astype(BF16)
    wa_bf, wb_bf, wo_bf = w_proj_a.astype(BF16), w_proj_b.astype(BF16), w_out.astype(BF16)

    n_c = B_P + B_S
    c_all = jnp.concatenate([c_prompt, c_sample, jnp.zeros((-n_c % SUB, D_MODEL), F32)], axis=0)
    mods = _mod_patterns(_ada(c_all, w_ada, b_ada))

    h0_re = state_s5_re.reshape(DEPTH, B_S, S5_N)
    h0_im = state_s5_im.reshape(DEPTH, B_S, S5_N)
    conv0 = state_conv.transpose(0, 2, 1, 3).reshape(DEPTH, (CONV_W - 1) * B_S, RG_WIDTH)
    gains = jnp.concatenate([norm_gain, final_gain[None]], axis=0).reshape(DEPTH + 1, 1, D_MODEL)

    xs_tm = x_sample.transpose(1, 0, 2).reshape(N_S, D_MODEL)
    x, xn = _embed(x_prompt, xs_tm, gains, mods)

    outs = {k: [] for k in ("s5re_p", "s5im_p", "h_p", "conv_p", "s5re_s", "s5im_s", "h_s", "conv_s")}
    y_prompt = y_sample = None
    half = SUB // 2
    xb_cols = slice(2 * S5_WIDTH, 2 * S5_WIDTH + RG_WIDTH)
    tail = CONV_W - 1
    for l in range(DEPTH):
        proj = _proj(xn, w_in, _lvec(b_in), l)
        ya, hp_s5, hs_re, hs_im = _s5_branch(
            proj, wbu, wc, a_re, a_im, _lvec(s5_d), s5_w_glu, _lvec(s5_b_glu), h0_re, h0_im, l)
        rg_params = (rg_conv_w, _lvec(rg_conv_b), wg, _lvec(rg_b_r), _lvec(rg_b_i), _lvec(rg_lam),
                     wa_bf, wb_bf)
        merged_p, hp = _rgm_prompt(proj, ya, *rg_params, l)
        merged_s, hs = _rgm_sample(proj, ya, *rg_params, state_rglru_h, conv0, l)
        res = _out_layer(merged_p, merged_s, x, wo_bf, mods, l, gains, l + 1)
        if l + 1 < DEPTH:
            x, xn = res
        else:
            y_prompt, y_sample = res[0], _batch_major(res[1], B_S, T_S)

        outs["s5re_p"].append(hp_s5[:half])
        outs["s5im_p"].append(hp_s5[half:])
        outs["h_p"].append(hp[half:])
        outs["conv_p"].append(proj[N_P - tail * B_P:N_P, xb_cols])
        outs["s5re_s"].append(hs_re)
        outs["s5im_s"].append(hs_im)
        outs["h_s"].append(hs)
        outs["conv_s"].append(proj[N_ROWS - tail * B_S:, xb_cols])

    st = lambda k: jnp.stack(outs[k]).astype(sdt)
    s5_shape = lambda b: (DEPTH, b, S5_GROUPS, S5_STATE)
    return (y_prompt, y_sample,
            st("s5re_p").reshape(s5_shape(B_P)), st("s5im_p").reshape(s5_shape(B_P)),
            st("h_p"), _batch_major(st("conv_p"), B_P, tail),
            st("s5re_s").reshape(s5_shape(B_S)), st("s5im_s").reshape(s5_shape(B_S)),
            st("h_s"), _batch_major(st("conv_s"), B_S, tail))
```
